```python
import jax, jax.numpy as jnp
from jax import lax
import numpy as np

D_MODEL = 1024
BATCH = 8
SEQ = 2048
DEPTH = 2
DEC_BATCH = 32
DEC_SEQ = 4
PAST_LEN = 8192
PAGE_SIZE = 128

D_MIX = D_MODEL
D_GROUP = D_MIX // 4
HEAD_DIM = 64
N_HEADS = D_GROUP // HEAD_DIM
GLA_DK = HEAD_DIM // 2
GLA_DV = HEAD_DIM
GLA_RANK = 16
GLA_TAU = 16.0
GLA_CHUNK = 64
SGU_CHUNK = 128
FOX_BLOCK = 128
FOX_BF_INIT = 7.0
CONV_WIDTH = 31
FFN_CONV_WIDTH = 3
D_FF = 2688
ALPHA = (2 * DEPTH) ** 0.25
BETA = (8 * DEPTH) ** -0.25
EPS = 1e-5
F32 = jnp.float32
IN_SPLIT = (N_HEADS * GLA_DK, N_HEADS * GLA_DK, N_HEADS * GLA_DV, N_HEADS * GLA_DV, GLA_RANK,
            D_GROUP, D_GROUP,
            D_GROUP, D_GROUP, D_GROUP, N_HEADS,
            2 * D_GROUP)
D_IN = sum(IN_SPLIT)

kernel_name = 'hybrid_gla_sgu_fox_conformer_decode_step'


def _layer_norm(x, g, b):
    xf = x.astype(F32)
    mu = jnp.mean(xf, -1, keepdims=True)
    var = jnp.mean(jnp.square(xf - mu), -1, keepdims=True)
    return ((xf - mu) * lax.rsqrt(var + EPS) * g + b).astype(x.dtype)


def _group_layer_norm(x, g, b, groups):
    shp = x.shape
    xg = x.reshape(*shp[:-1], groups, shp[-1] // groups)
    return _layer_norm(xg, g.reshape(groups, -1), b.reshape(groups, -1)).reshape(shp)


def _causal_dwconv(x, buf, w, b):
    xx = jnp.concatenate([buf.astype(x.dtype), x], axis=1)
    y = lax.conv_general_dilated(xx, w[:, None, :].astype(xx.dtype), window_strides=(1,), padding='VALID',
                                 dimension_numbers=('NWC', 'WIO', 'NWC'), feature_group_count=x.shape[-1])
    return y + b, xx[:, -(w.shape[0] - 1):]


def _gla(q, k, v, log_a, s0, chunk):
    B, T, H, _ = q.shape
    n = T // chunk

    def rs(t):
        return t.astype(F32).reshape(B, n, chunk, H, -1).transpose(1, 0, 3, 2, 4)

    qc, kc, vc = rs(q), rs(k), rs(v)
    gc = jnp.cumsum(rs(log_a), axis=3)
    causal = jnp.tril(jnp.ones((chunk, chunk), bool))

    def step(S, inp):
        qi, ki, vi, gi = inp
        diff = gi[..., :, None, :] - gi[..., None, :, :]
        decay = jnp.exp(jnp.where(causal[:, :, None], diff, -jnp.inf))
        attn = jnp.einsum('bhtd,bhsd,bhtsd->bhts', qi, ki, decay)
        o = jnp.einsum('bhtd,bhde->bhte', qi * jnp.exp(gi), S) + jnp.einsum('bhts,bhse->bhte', attn, vi)
        g_last = gi[..., -1:, :]
        S = S * jnp.exp(g_last[..., 0, :])[..., None] + jnp.einsum('bhsd,bhse->bhde', ki * jnp.exp(g_last - gi), vi)
        return S, o

    S, o = lax.scan(step, s0.astype(F32), (qc, kc, vc, gc))
    return o.transpose(1, 0, 3, 2, 4).reshape(B, T, H, -1), S


def _sgu(u, v, w_s, b_s, chunk):
    B, T, _ = u.shape
    n = T // chunk
    w = jnp.tril(w_s[:, :chunk, :chunk])
    vc = v.reshape(B, n, chunk, N_HEADS, -1)
    mixed = jnp.einsum('gts,bnsgc->bntgc', w, vc) + b_s[:, :chunk].T[None, None, :, :, None]
    return u * mixed.reshape(B, T, -1)


def _fox_prompt(q, k, v, logf):
    B, T, H, D = q.shape
    dcum = jnp.cumsum(logf, axis=1).transpose(0, 2, 1)
    key_pos = jnp.arange(T)
    scale = D ** -0.5

    def block(i):
        start = i * FOX_BLOCK
        q_blk = lax.dynamic_slice_in_dim(q, start, FOX_BLOCK, axis=1)
        d_blk = lax.dynamic_slice_in_dim(dcum, start, FOX_BLOCK, axis=2)
        s = jnp.einsum('bqhd,bkhd->bhqk', q_blk, k, preferred_element_type=F32) * scale
        s = s + d_blk[..., :, None] - dcum[..., None, :]
        q_pos = start + jnp.arange(FOX_BLOCK)
        s = jnp.where(key_pos[None, :] <= q_pos[:, None], s, -jnp.inf)
        p = jax.nn.softmax(s, axis=-1)
        return jnp.einsum('bhqk,bkhd->bqhd', p.astype(v.dtype), v)

    o = lax.map(block, jnp.arange(T // FOX_BLOCK))
    return o.transpose(1, 0, 2, 3, 4).reshape(B, T, H, D)


def _fox_sample(q, k, v, logf, k_past, v_past, logf_past):
    B, S, H, D = q.shape
    P = k_past.shape[1]
    lp = logf_past.astype(F32)
    suffix = lax.cumsum(lp, axis=1, reverse=True) - lp
    d_new = jnp.cumsum(logf, axis=1)
    key_bias = jnp.concatenate([suffix, -d_new], axis=1).transpose(0, 2, 1)
    k_all = jnp.concatenate([k_past.astype(k.dtype), k], axis=1)
    v_all = jnp.concatenate([v_past.astype(v.dtype), v], axis=1)
    s = jnp.einsum('bqhd,bkhd->bhqk', q, k_all, preferred_element_type=F32) * D ** -0.5
    s = s + d_new.transpose(0, 2, 1)[..., :, None] + key_bias[..., None, :]
    key_pos = jnp.arange(P + S)
    q_pos = P + jnp.arange(S)
    s = jnp.where(key_pos[None, :] <= q_pos[:, None], s, -jnp.inf)
    p = jax.nn.softmax(s, axis=-1)
    return jnp.einsum('bhqk,bkhd->bqhd', p.astype(v_all.dtype), v_all)


def _mixer(x, p, gla_s0, conv_buf, fox_past, chunk_a, chunk_b):
    B, T, _ = x.shape
    h = jnp.einsum('btd,de->bte', x, p['w_in'])
    points = [int(c) for c in np.cumsum(IN_SPLIT)[:-1]]
    (a_q, a_k, a_v, a_g, a_lr, b_u, b_v, c_q, c_k, c_v, c_f, d_in) = jnp.split(h, points, axis=-1)
    q = a_q.reshape(B, T, N_HEADS, GLA_DK) * GLA_DK ** -0.5
    k = a_k.reshape(B, T, N_HEADS, GLA_DK)
    v = a_v.reshape(B, T, N_HEADS, GLA_DV)
    log_a = jax.nn.log_sigmoid((jnp.einsum('btr,re->bte', a_lr, p['gla_w_a']) + p['gla_b_a']).astype(F32)) / GLA_TAU
    o_a, gla_state = _gla(q, k, v, log_a.reshape(B, T, N_HEADS, GLA_DK), gla_s0, chunk_a)
    o_a = o_a * lax.rsqrt(jnp.mean(jnp.square(o_a), -1, keepdims=True) + EPS)
    y_a = (o_a.reshape(B, T, -1) * p['gla_norm_g']).astype(x.dtype) * jax.nn.silu(a_g)
    v_ln = _group_layer_norm(b_v, p['sgu_ln_g'], p['sgu_ln_b'], N_HEADS)
    y_b = _sgu(b_u, v_ln, p['sgu_w'], p['sgu_b'], chunk_b)
    fq = c_q.reshape(B, T, N_HEADS, HEAD_DIM)
    fk = c_k.reshape(B, T, N_HEADS, HEAD_DIM)
    fv = c_v.reshape(B, T, N_HEADS, HEAD_DIM)
    logf = jax.nn.log_sigmoid((c_f + p['fox_b_f']).astype(F32))
    if fox_past is None:
        o_c = _fox_prompt(fq, fk, fv, logf)
    else:
        o_c = _fox_sample(fq, fk, fv, logf, *fox_past)
    y_c = o_c.reshape(B, T, -1)
    glu = d_in[..., :D_GROUP] * jax.nn.sigmoid(d_in[..., D_GROUP:])
    conv_out, conv_state = _causal_dwconv(glu, conv_buf, p['conv_w'], p['conv_b'])
    y_d = jax.nn.silu(_group_layer_norm(conv_out, p['conv_norm_g'], p['conv_norm_b'], N_HEADS))
    y = jnp.concatenate([y_a, y_b, y_c.astype(x.dtype), y_d], axis=-1)
    out = jnp.einsum('bte,ed->btd', y, p['w_o'])
    return out, (fk, fv, logf, gla_state, conv_state, v_ln)


def _ffn(x, p, buf):
    h = jnp.einsum('btd,df->btf', x, p['ffn_w_up'])
    gate, val = h[..., :D_FF], h[..., D_FF:]
    gate_c, new_buf = _causal_dwconv(gate, buf, p['ffn_conv_w'], p['ffn_conv_b'])
    return jnp.einsum('btf,fd->btd', jax.nn.silu(gate_c) * val, p['ffn_w_down']), new_buf


def _layer(x, p, gla_s0, conv_buf, ffn_buf, fox_past, chunk_a, chunk_b):
    m, st = _mixer(x, p, gla_s0, conv_buf, fox_past, chunk_a, chunk_b)
    x = _layer_norm(ALPHA * x + m, p['ln1_g'], p['ln1_b'])
    f, ffn_state = _ffn(x, p, ffn_buf)
    x = _layer_norm(ALPHA * x + f, p['ln2_g'], p['ln2_b'])
    return x, st + (ffn_state,)


def _gather_pages(pool, page_table):
    g = pool[page_table]
    return g.reshape(page_table.shape[0], -1, *pool.shape[2:])


def _stack(states, i):
    return jnp.stack([s[i] for s in states])


def setup_inputs(seed: int = 0) -> dict:
    key = jax.random.key(seed)
    ks = iter(jax.random.split(key, 48))

    def nrm(shape, scale):
        return jax.random.normal(next(ks), shape, F32) * scale

    n_pages = PAST_LEN // PAGE_SIZE
    n_used = DEC_BATCH * n_pages
    n_pool = n_used + n_used // 4
    return {
        'x_prompt': nrm((BATCH, SEQ, D_MODEL), 1.0),
        'x_sample': nrm((DEC_BATCH, DEC_SEQ, D_MODEL), 1.0),
        'cache_fox_k': nrm((DEPTH, n_pool, PAGE_SIZE, N_HEADS, HEAD_DIM), 1.0),
        'cache_fox_v': nrm((DEPTH, n_pool, PAGE_SIZE, N_HEADS, HEAD_DIM), 1.0),
        'cache_fox_logf': jax.nn.log_sigmoid(FOX_BF_INIT + nrm((DEPTH, n_pool, PAGE_SIZE, N_HEADS), 0.5)),
        'state_gla': nrm((DEPTH, DEC_BATCH, N_HEADS, GLA_DK, GLA_DV), 1.0),
        'state_conv': nrm((DEPTH, DEC_BATCH, CONV_WIDTH - 1, D_GROUP), 0.5),
        'state_ffn_conv': nrm((DEPTH, DEC_BATCH, FFN_CONV_WIDTH - 1, D_FF), 1.0),
        'page_table': jax.random.permutation(next(ks), n_pool)[:n_used].reshape(DEC_BATCH, n_pages).astype(jnp.int32),
        'w_in': nrm((DEPTH, D_MODEL, D_IN), D_MODEL ** -0.5),
        'gla_w_a': nrm((DEPTH, GLA_RANK, N_HEADS * GLA_DK), GLA_RANK ** -0.5),
        'gla_b_a': nrm((DEPTH, N_HEADS * GLA_DK), 0.1),
        'gla_norm_g': 1.0 + nrm((DEPTH, N_HEADS * GLA_DV), 0.1),
        'sgu_ln_g': 1.0 + nrm((DEPTH, D_GROUP), 0.1),
        'sgu_ln_b': nrm((DEPTH, D_GROUP), 0.01),
        'sgu_w': nrm((DEPTH, N_HEADS, SGU_CHUNK, SGU_CHUNK), SGU_CHUNK ** -0.5),
        'sgu_b': 1.0 + nrm((DEPTH, N_HEADS, SGU_CHUNK), 0.1),
        'fox_b_f': FOX_BF_INIT + nrm((DEPTH, N_HEADS), 0.5),
        'conv_w': nrm((DEPTH, CONV_WIDTH, D_GROUP), CONV_WIDTH ** -0.5),
        'conv_b': nrm((DEPTH, D_GROUP), 0.01),
        'conv_norm_g': 1.0 + nrm((DEPTH, D_GROUP), 0.1),
        'conv_norm_b': nrm((DEPTH, D_GROUP), 0.01),
        'w_o': nrm((DEPTH, D_MIX, D_MODEL), BETA * D_MIX ** -0.5),
        'ln1_g': 1.0 + nrm((DEPTH, D_MODEL), 0.1),
        'ln1_b': nrm((DEPTH, D_MODEL), 0.01),
        'ffn_w_up': nrm((DEPTH, D_MODEL, 2 * D_FF), D_MODEL ** -0.5),
        'ffn_conv_w': nrm((DEPTH, FFN_CONV_WIDTH, D_FF), FFN_CONV_WIDTH ** -0.5),
        'ffn_conv_b': nrm((DEPTH, D_FF), 0.01),
        'ffn_w_down': nrm((DEPTH, D_FF, D_MODEL), BETA * D_FF ** -0.5),
        'ln2_g': 1.0 + nrm((DEPTH, D_MODEL), 0.1),
        'ln2_b': nrm((DEPTH, D_MODEL), 0.01),
    }


def reference(x_prompt, x_sample, cache_fox_k, cache_fox_v, cache_fox_logf, state_gla, state_conv,
              state_ffn_conv, page_table, w_in, gla_w_a, gla_b_a, gla_norm_g, sgu_ln_g, sgu_ln_b, sgu_w,
              sgu_b, fox_b_f, conv_w, conv_b, conv_norm_g, conv_norm_b, w_o, ln1_g, ln1_b, ffn_w_up,
              ffn_conv_w, ffn_conv_b, ffn_w_down, ln2_g, ln2_b):
    B = x_prompt.shape[0]
    S = x_sample.shape[1]
    xp, xs = x_prompt, x_sample
    st_p, st_s = [], []
    for l in range(DEPTH):
        p = dict(w_in=w_in[l], gla_w_a=gla_w_a[l], gla_b_a=gla_b_a[l], gla_norm_g=gla_norm_g[l],
                 sgu_ln_g=sgu_ln_g[l], sgu_ln_b=sgu_ln_b[l], sgu_w=sgu_w[l], sgu_b=sgu_b[l],
                 fox_b_f=fox_b_f[l], conv_w=conv_w[l], conv_b=conv_b[l], conv_norm_g=conv_norm_g[l],
                 conv_norm_b=conv_norm_b[l], w_o=w_o[l], ln1_g=ln1_g[l], ln1_b=ln1_b[l],
                 ffn_w_up=ffn_w_up[l], ffn_conv_w=ffn_conv_w[l], ffn_conv_b=ffn_conv_b[l],
                 ffn_w_down=ffn_w_down[l], ln2_g=ln2_g[l], ln2_b=ln2_b[l])
        xp, sp = _layer(xp, p,
                        jnp.zeros((B, N_HEADS, GLA_DK, GLA_DV), F32),
                        jnp.zeros((B, CONV_WIDTH - 1, D_GROUP), xp.dtype),
                        jnp.zeros((B, FFN_CONV_WIDTH - 1, D_FF), xp.dtype),
                        None, GLA_CHUNK, SGU_CHUNK)
        fox_past = (_gather_pages(cache_fox_k[l], page_table),
                    _gather_pages(cache_fox_v[l], page_table),
                    _gather_pages(cache_fox_logf[l], page_table))
        xs, ss = _layer(xs, p, state_gla[l], state_conv[l], state_ffn_conv[l], fox_past, S, S)
        st_p.append(sp)
        st_s.append(ss)
    p_fox_k, p_fox_v, p_fox_logf = _stack(st_p, 0), _stack(st_p, 1), _stack(st_p, 2)
    p_gla, p_conv, p_ffn_conv = _stack(st_p, 3), _stack(st_p, 4), _stack(st_p, 6)
    s_fox_k, s_fox_v, s_fox_logf = _stack(st_s, 0), _stack(st_s, 1), _stack(st_s, 2)
    s_gla, s_conv, s_sgu_v, s_ffn_conv = _stack(st_s, 3), _stack(st_s, 4), _stack(st_s, 5), _stack(st_s, 6)
    return (xp, xs, p_fox_k, p_fox_v, p_fox_logf, p_gla, p_conv, p_ffn_conv,
            s_fox_k, s_fox_v, s_fox_logf, s_gla, s_conv, s_ffn_conv, s_sgu_v)
```

```python
import functools

import numpy as np
import jax
import jax.numpy as jnp
from jax import lax
from jax.experimental import pallas as pl
from jax.experimental.pallas import tpu as pltpu

F32 = jnp.float32
BF16 = jnp.bfloat16

N_HEADS = 4
HEAD_DIM = 64
D_GROUP = N_HEADS * HEAD_DIM
GLA_DK = 32
GLA_LANES = N_HEADS * GLA_DK
GLA_RANK = 16
GLA_TAU = 16.0
CONV_WIDTH = 31
CONV_PAD = 32
FFN_CONV_WIDTH = 3
FFN_PAD = 8
EPS = 1e-5
LANE = 128
PAGE = 128
SAMPLE_ROWS = 16
VMEM_LIMIT = 56 * 1024 * 1024

_SEGS = (("aq", 128, 128), ("ak", 128, 128), ("av", 256, 256), ("ag", 256, 256), ("alr", 16, 128),
         ("bu", 256, 256), ("bv", 256, 256), ("cq", 256, 256), ("ck", 256, 256), ("cv", 256, 256),
         ("cf", 4, 128), ("d1", 256, 256), ("d2", 256, 256))
_OFF = {}
_o = 0
for _n, _w, _p in _SEGS:
    _OFF[_n] = _o
    _o += _p
D_IN_PAD = _o


def _cols(name, width):
    return slice(_OFF[name], _OFF[name] + width)


def _pack_w_in(w_in):
    parts, src = [], 0
    for _, w, p in _SEGS:
        blk = w_in[:, src:src + w]
        if p > w:
            blk = jnp.pad(blk, ((0, 0), (0, p - w)))
        parts.append(blk)
        src += w
    return jnp.concatenate(parts, axis=1).astype(BF16)


def _split_dot_rhs(x, m, passes=3):
    acc, r = None, x
    for p in range(passes):
        hi = r.astype(BF16)
        d = jnp.dot(hi, m, preferred_element_type=F32)
        acc = d if acc is None else acc + d
        if p + 1 < passes:
            r = r - hi.astype(F32)
    return acc


def _split_dot_lhs(m, x, passes=3):
    acc, r = None, x
    for p in range(passes):
        hi = r.astype(BF16)
        d = jnp.dot(m, hi, preferred_element_type=F32)
        acc = d if acc is None else acc + d
        if p + 1 < passes:
            r = r - hi.astype(F32)
    return acc


_NT = (((1,), (1,)), ((), ()))
_TN = (((0,), (0,)), ((), ()))


def _sigmoid(x):
    return 1.0 / (1.0 + jnp.exp(-x))


def _silu(x):
    return x * _sigmoid(x)


def _log_sigmoid(x):
    return jnp.minimum(x, 0.0) - jnp.log1p(jnp.exp(-jnp.abs(x)))


def _group_norm(x, ones_bd, g, b):
    inv = 1.0 / HEAD_DIM
    mu = _split_dot_rhs(x, ones_bd) * inv
    xc = x - mu
    var = _split_dot_rhs(xc * xc, ones_bd) * inv
    return xc * lax.rsqrt(var + EPS) * g + b


def _layer_norm_rows(y, g, b):
    mu = jnp.mean(y, axis=-1, keepdims=True)
    yc = y - mu
    var = jnp.mean(yc * yc, axis=-1, keepdims=True)
    return yc * lax.rsqrt(var + EPS) * g + b


def _gla_consts(L):
    K = L.bit_length() - 1
    assert 1 << K == L
    t = np.arange(L)
    rows = []
    for l in range(K + 1):
        m = (t >> l) << l
        rows.append((t[None, :] >= m[:, None]) & (t[None, :] <= t[:, None]))
    for l in range(K + 1):
        mp = ((t >> l) + 1) << l
        rows.append((t[None, :] >= t[:, None] + 1) & (t[None, :] <= mp[:, None] - 1))
    big = np.concatenate(rows, 0).astype(np.float32)
    x = t[:, None] ^ t[None, :]
    lev = np.where(x > 0, np.floor(np.log2(np.maximum(x, 1))).astype(np.int32), -1)
    lev = np.where(t[None, :] > t[:, None], -2, lev).astype(np.int32)
    return big, np.tile(lev, (1, N_HEADS))


def _head_mask(rows_per_head, lanes_per_head):
    r = np.arange(N_HEADS * rows_per_head)[:, None] // rows_per_head
    c = np.arange(N_HEADS * lanes_per_head)[None, :] // lanes_per_head
    return (r == c).astype(np.float32)


def _mixer_kernel(x_ref, w_in_ref, w_a_ref, b_a_ref, gn_ref, sgu_g_ref, sgu_b_ref, wcat_ref, sbias_ref,
                  bf_ref, cw_ref, cb_ref, cng_ref, cnb_ref,
                  big_ref, lev_ref, hmk_ref, hmv_ref, hms_ref, sbd_ref, ones_ref, tril_ref,
                  s0_gla_ref, s0_conv_ref,
                  y_ref, fq_ref, fk_ref, fv_ref, fkb_ref, fvb_ref, logf_ref, dq_ref, dkt_ref,
                  gla_out_ref, conv_out_ref, vln_ref,
                  h_scr, gla_scr, xx_scr, dc_scr,
                  *, BB, TB, LG, LS, T_VALID_LAST, KLEV, KV_T):
    j = pl.program_id(1)
    nj = pl.num_programs(1)
    R = BB * TB

    @pl.when(j == 0)
    def _():
        gla_scr[...] = s0_gla_ref[...]
        xx_scr[:, 0:CONV_PAD, :] = s0_conv_ref[...]
        dc_scr[...] = jnp.zeros_like(dc_scr)

    xb = x_ref[...].reshape(R, x_ref.shape[-1]).astype(BF16)
    h_scr[...] = jnp.dot(xb, w_in_ref[...], preferred_element_type=F32)

    ones_bd = ones_ref[...]
    lev = lev_ref[...]
    hmk = hmk_ref[...]
    hmv = hmv_ref[...]
    hms = hms_ref[...]
    sbd = sbd_ref[...]
    big = big_ref[...]
    row_t = lax.broadcasted_iota(jnp.int32, (TB, 1), 0)
    valid = row_t < T_VALID_LAST

    def seq_body(bb, carry):
        r0 = 0 if BB == 1 else pl.multiple_of(bb * TB, TB)

        def hs(c0, n, name, width):
            return h_scr[pl.ds(r0 + c0, n), _cols(name, width)]

        S = gla_scr[bb]
        for c in range(TB // LG):
            c0 = c * LG
            q = hs(c0, LG, "aq", 128) * (GLA_DK ** -0.5)
            k = hs(c0, LG, "ak", 128)
            v = hs(c0, LG, "av", 256)
            lr = hs(c0, LG, "alr", 128).astype(BF16)
            xa = jnp.dot(lr, w_a_ref[...], preferred_element_type=F32) + b_a_ref[...]
            la = _log_sigmoid(xa) * (1.0 / GLA_TAU)
            if T_VALID_LAST < TB:
                vm = valid[c0:c0 + LG]
                la = jnp.where(vm, la, 0.0)
                k = jnp.where(vm, k, 0.0)
            X = jnp.exp(_split_dot_lhs(big, la))

            def qs(l):
                return X[l * LG:(l + 1) * LG]

            def ks(l):
                return X[(KLEV + 1 + l) * LG:(KLEV + 2 + l) * LG]

            def kstack(kk):
                kb = kk.astype(BF16)
                return jnp.concatenate([kb] * N_HEADS, axis=0) * hmk

            Rl = lax.dot_general(q.astype(BF16), kstack(k), _NT, preferred_element_type=F32)
            attn = jnp.where(lev == -1, Rl, 0.0)
            for l in range(KLEV):
                Rl = lax.dot_general((q * qs(l)).astype(BF16), kstack(k * ks(l)), _NT,
                                     preferred_element_type=F32)
                attn = jnp.where(lev == l, Rl, attn)
            vb = v.astype(BF16)
            vbd = jnp.concatenate([vb] * N_HEADS, axis=0) * hmv
            o = jnp.dot(attn.astype(BF16), vbd, preferred_element_type=F32)
            o = o + lax.dot_general((q * qs(KLEV)).astype(BF16), S.astype(BF16), _NT,
                                    preferred_element_type=F32)
            khat = (k * ks(KLEV)).astype(BF16)
            upd = lax.dot_general(vb, khat, _TN, preferred_element_type=F32)
            decay = X[(KLEV + 1) * LG - 1:(KLEV + 1) * LG]
            S = S * decay + upd * sbd
            ms = _split_dot_rhs(o * o, ones_bd) * (1.0 / HEAD_DIM)
            o = o * lax.rsqrt(ms + EPS) * gn_ref[...]
            y_a = o * _silu(hs(c0, LG, "ag", 256))
            y_ref[bb, c0:c0 + LG, 0:256] = y_a.astype(y_ref.dtype)
        gla_scr[bb] = S

        for c in range(TB // LS):
            c0 = c * LS
            v_ln = _group_norm(hs(c0, LS, "bv", 256), ones_bd, sgu_g_ref[...], sgu_b_ref[...])
            if vln_ref is not None:
                vln_ref[bb, c0:c0 + LS, :] = v_ln
            vbd = jnp.concatenate([v_ln.astype(BF16)] * N_HEADS, axis=0) * hms
            mixed = jnp.dot(wcat_ref[...], vbd, preferred_element_type=F32) + sbias_ref[...]
            y_b = hs(c0, LS, "bu", 256) * mixed
            y_ref[bb, c0:c0 + LS, 256:512] = y_b.astype(y_ref.dtype)

        glu = hs(0, TB, "d1", 256) * _sigmoid(hs(0, TB, "d2", 256))
        xx_scr[bb, CONV_PAD:CONV_PAD + TB, :] = glu
        CR = min(TB, 64)
        for c in range(TB // CR):
            c0 = c * CR
            acc = jnp.zeros((CR, D_GROUP), F32) + cb_ref[...]
            for tap in range(CONV_WIDTH):
                s = c0 + CONV_PAD - (CONV_WIDTH - 1) + tap
                acc = acc + xx_scr[bb, s:s + CR, :] * cw_ref[tap:tap + 1, :]
            y_d = _silu(_group_norm(acc, ones_bd, cng_ref[...], cnb_ref[...]))
            y_ref[bb, c0:c0 + CR, 512:768] = y_d.astype(y_ref.dtype)
        tail0 = T_VALID_LAST + CONV_PAD - (CONV_WIDTH - 1)
        conv_out_ref[bb] = xx_scr[bb, tail0:tail0 + CONV_WIDTH - 1, :]
        xx_scr[bb, 0:CONV_PAD, :] = xx_scr[bb, TB:TB + CONV_PAD, :]

        fq_ref[bb] = (hs(0, TB, "cq", 256) * (HEAD_DIM ** -0.5)).astype(BF16)
        fk = hs(0, TB, "ck", 256)
        fv = hs(0, TB, "cv", 256)
        fk_ref[bb] = fk.T if KV_T else fk
        fv_ref[bb] = fv.T if KV_T else fv
        fkb_ref[bb] = fk.astype(BF16)
        fvb_ref[bb] = fv.astype(BF16)
        logf = _log_sigmoid(hs(0, TB, "cf", 128) + bf_ref[...])
        logf_ref[bb] = logf[:, 0:N_HEADS]
        dcum = _split_dot_lhs(tril_ref[...], logf) + dc_scr[bb, 0:1, :]
        dq_ref[bb] = dcum
        dkt_ref[bb, 0] = dcum.T[0:8, :]
        dc_scr[bb] = jnp.broadcast_to(dcum[TB - 1:TB, :], (8, LANE))
        return carry

    if BB == 1:
        seq_body(0, 0)
    else:
        lax.fori_loop(0, BB, seq_body, 0)
    gla_out_ref[...] = gla_scr[...]
    del nj


def _mixer_call(x, wts, s0_gla, s0_conv, *, BB, TB, LG, LS, t_valid_last, want_vln, kv_t):
    B, T, D = x.shape
    assert B % BB == 0 and T % TB == 0 and TB % LG == 0 and TB % LS == 0
    klev = LG.bit_length() - 1
    big, lev = _gla_consts(LG)
    consts = [jnp.asarray(big, BF16), jnp.asarray(lev), jnp.asarray(_head_mask(LG, GLA_DK), BF16),
              jnp.asarray(_head_mask(LG, HEAD_DIM), BF16), jnp.asarray(_head_mask(LS, HEAD_DIM), BF16),
              jnp.asarray(_head_mask(HEAD_DIM, GLA_DK), F32), jnp.asarray(_head_mask(HEAD_DIM, HEAD_DIM), BF16),
              jnp.asarray(np.tril(np.ones((TB, TB), np.float32)), BF16)]
    params = [wts["w_in"], wts["w_a"], wts["b_a"], wts["gn"], wts["sgu_g"], wts["sgu_b"],
              wts["wcat"][LS], wts["sbias"][LS], wts["bf"], wts["cw"], wts["cb"], wts["cng"], wts["cnb"]]

    def full(a):
        return pl.BlockSpec(a.shape, lambda i, j, _n=a.ndim: (0,) * _n)

    def seq3(width):
        return pl.BlockSpec((BB, TB, width), lambda i, j: (i, j, 0))

    def per_b(shape):
        return pl.BlockSpec((BB,) + shape, lambda i, j: (i,) + (0,) * len(shape))

    in_specs = [seq3(D)] + [full(a) for a in params] + [full(a) for a in consts] + \
               [per_b(s0_gla.shape[1:]), per_b(s0_conv.shape[1:])]
    out_shape = [jax.ShapeDtypeStruct((B, T, 3 * D_GROUP), BF16),
                 jax.ShapeDtypeStruct((B, T, D_GROUP), BF16),
                 jax.ShapeDtypeStruct((B, D_GROUP, T) if kv_t else (B, T, D_GROUP), F32),
                 jax.ShapeDtypeStruct((B, D_GROUP, T) if kv_t else (B, T, D_GROUP), F32),
                 jax.ShapeDtypeStruct((B, T, D_GROUP), BF16),
                 jax.ShapeDtypeStruct((B, T, D_GROUP), BF16),
                 jax.ShapeDtypeStruct((B, T, N_HEADS), F32),
                 jax.ShapeDtypeStruct((B, T, LANE), F32),
                 jax.ShapeDtypeStruct((B, T // TB, 8, TB), F32),
                 jax.ShapeDtypeStruct((B, D_GROUP, GLA_LANES), F32),
                 jax.ShapeDtypeStruct((B, CONV_WIDTH - 1, D_GROUP), F32)]
    kv_spec = pl.BlockSpec((BB, D_GROUP, TB), lambda i, j: (i, 0, j)) if kv_t else seq3(D_GROUP)
    out_specs = [seq3(3 * D_GROUP), seq3(D_GROUP), kv_spec, kv_spec, seq3(D_GROUP), seq3(D_GROUP),
                 seq3(N_HEADS), seq3(LANE),
                 pl.BlockSpec((BB, 1, 8, TB), lambda i, j: (i, j, 0, 0)),
                 per_b((D_GROUP, GLA_LANES)), per_b((CONV_WIDTH - 1, D_GROUP))]
    if want_vln:
        out_shape.append(jax.ShapeDtypeStruct((B, T, D_GROUP), F32))
        out_specs.append(seq3(D_GROUP))

    def body(*refs):
        n_in = len(in_specs)
        n_out = len(out_specs)
        ins, outs, scr = refs[:n_in], list(refs[n_in:n_in + n_out]), refs[n_in + n_out:]
        if not want_vln:
            outs.append(None)
        _mixer_kernel(*ins, *outs, *scr, BB=BB, TB=TB, LG=LG, LS=LS, T_VALID_LAST=t_valid_last, KLEV=klev,
                      KV_T=kv_t)

    return pl.pallas_call(
        body,
        grid=(B // BB, T // TB),
        in_specs=in_specs,
        out_specs=out_specs,
        out_shape=out_shape,
        scratch_shapes=[pltpu.VMEM((BB * TB, D_IN_PAD), F32),
                        pltpu.VMEM((BB, D_GROUP, GLA_LANES), F32),
                        pltpu.VMEM((BB, CONV_PAD + TB, D_GROUP), F32),
                        pltpu.VMEM((BB, 8, LANE), F32)],
        compiler_params=pltpu.CompilerParams(dimension_semantics=("arbitrary", "arbitrary"),
                                             vmem_limit_bytes=VMEM_LIMIT),
        name="mixer_pre",
    )(x, *params, *consts, s0_gla, s0_conv)


def _flash_kernel(q_ref, k_ref, v_ref, dq_ref, dkt_ref, o_ref, *, TQ):
    i = pl.program_id(1)
    q = q_ref[0]
    lane_h = lax.broadcasted_iota(jnp.int32, (1, D_GROUP), 1) // HEAD_DIM
    qh = [jnp.where(lane_h == h, q, jnp.zeros_like(q)) for h in range(N_HEADS)]
    dq = dq_ref[0]
    dqh = [dq[:, h:h + 1] for h in range(N_HEADS)]
    row = lax.broadcasted_iota(jnp.int32, (TQ, TQ), 0)
    col = lax.broadcasted_iota(jnp.int32, (TQ, TQ), 1)

    def step(jb, carry, masked):
        ms, ls, acc = carry
        start = pl.multiple_of(jb * TQ, TQ)
        k = k_ref[0, pl.ds(start, TQ), :]
        v = v_ref[0, pl.ds(start, TQ), :]
        dk = dkt_ref[0, jb]
        new_ms, new_ls = [], []
        scale = jnp.zeros((TQ, D_GROUP), F32)
        pv_all = jnp.zeros((TQ, D_GROUP), F32)
        for h in range(N_HEADS):
            s = lax.dot_general(qh[h], k, _NT, preferred_element_type=F32)
            s = s + dqh[h] - dk[h:h + 1, :]
            if masked:
                s = jnp.where(col <= row, s, -jnp.inf)
            m_new = jnp.maximum(ms[h], jnp.max(s, axis=-1, keepdims=True))
            alpha = jnp.exp(ms[h] - m_new)
            p = jnp.exp(s - m_new)
            new_ls.append(alpha * ls[h] + jnp.sum(p, axis=-1, keepdims=True))
            new_ms.append(m_new)
            pv = jnp.dot(p.astype(BF16), v, preferred_element_type=F32)
            pv_all = jnp.where(lane_h == h, pv, pv_all)
            scale = jnp.where(lane_h == h, alpha, scale)
        return tuple(new_ms), tuple(new_ls), acc * scale + pv_all

    init = (tuple(jnp.full((TQ, 1), -jnp.inf, F32) for _ in range(N_HEADS)),
            tuple(jnp.zeros((TQ, 1), F32) for _ in range(N_HEADS)),
            jnp.zeros((TQ, D_GROUP), F32))
    carry = lax.fori_loop(0, i, lambda jb, c: step(jb, c, False), init)
    ms, ls, acc = step(i, carry, True)
    denom = jnp.zeros((TQ, D_GROUP), F32)
    for h in range(N_HEADS):
        denom = jnp.where(lane_h == h, ls[h], denom)
    o_ref[0] = (acc / denom).astype(o_ref.dtype)


def _flash_call(fq, fkb, fvb, dq, dkt, *, TQ):
    B, T, _ = fq.shape
    return pl.pallas_call(
        functools.partial(_flash_kernel, TQ=TQ),
        grid=(B, T // TQ),
        in_specs=[pl.BlockSpec((1, TQ, D_GROUP), lambda b, i: (b, i, 0)),
                  pl.BlockSpec((1, T, D_GROUP), lambda b, i: (b, 0, 0)),
                  pl.BlockSpec((1, T, D_GROUP), lambda b, i: (b, 0, 0)),
                  pl.BlockSpec((1, TQ, LANE), lambda b, i: (b, i, 0)),
                  pl.BlockSpec((1, T // TQ, 8, TQ), lambda b, i: (b, 0, 0, 0))],
        out_specs=pl.BlockSpec((1, TQ, D_GROUP), lambda b, i: (b, i, 0)),
        out_shape=jax.ShapeDtypeStruct((B, T, D_GROUP), BF16),
        compiler_params=pltpu.CompilerParams(dimension_semantics=("arbitrary", "arbitrary"),
                                             vmem_limit_bytes=VMEM_LIMIT),
        name="fox_prompt",
    )(fq, fkb, fvb, dq, dkt)


def _logf_prep_kernel(lp_ref, u_ref, te_ref, within_ref, tot_ref):
    lp = lp_ref[...]
    within_ref[...] = _split_dot_rhs(lp, u_ref[...])
    tot_ref[...] = _split_dot_rhs(lp, te_ref[...])


def _logf_prep_call(lp2d):
    n_pool, W = lp2d.shape
    r = np.arange(W)
    h_in, s_in = r // PAGE, r % PAGE
    h_out, s_out = r // PAGE, r % PAGE
    same_h = h_in[:, None] == h_out[None, :]
    u = jnp.asarray((same_h & (s_in[:, None] > s_out[None, :])).astype(np.float32), BF16)
    te = jnp.asarray(same_h.astype(np.float32), BF16)
    RB = n_pool
    for cand in (640, 512, 256, 128, 64, 32, 16, 8):
        if n_pool % cand == 0:
            RB = cand
            break
    return pl.pallas_call(
        _logf_prep_kernel,
        grid=(n_pool // RB,),
        in_specs=[pl.BlockSpec((RB, W), lambda i: (i, 0)),
                  pl.BlockSpec((W, W), lambda i: (0, 0)),
                  pl.BlockSpec((W, W), lambda i: (0, 0))],
        out_specs=[pl.BlockSpec((RB, W), lambda i: (i, 0)), pl.BlockSpec((RB, W), lambda i: (i, 0))],
        out_shape=[jax.ShapeDtypeStruct((n_pool, W), F32), jax.ShapeDtypeStruct((n_pool, W), F32)],
        compiler_params=pltpu.CompilerParams(dimension_semantics=("arbitrary",), vmem_limit_bytes=VMEM_LIMIT),
        name="logf_suffix",
    )(lp2d, u, te)


def _paged_kernel(pt_ref, q_ref, dq_ref, kn_ref, vn_ref, dnt_ref, *refs, PG, T_VALID):
    k_refs = refs[0:PG]
    v_refs = refs[PG:2 * PG]
    w_refs = refs[2 * PG:3 * PG]
    t_refs = refs[3 * PG:4 * PG]
    o_ref = refs[4 * PG]
    m_scr, l_scr, acc_scr, carry_scr = refs[4 * PG + 1:]
    del pt_ref
    g = pl.program_id(1)
    ng = pl.num_programs(1)
    QR = 8

    @pl.when(g == 0)
    def _():
        m_scr[...] = jnp.full_like(m_scr, -jnp.inf)
        l_scr[...] = jnp.zeros_like(l_scr)
        acc_scr[...] = jnp.zeros_like(acc_scr)
        carry_scr[...] = jnp.zeros_like(carry_scr)

    lane_h = lax.broadcasted_iota(jnp.int32, (1, D_GROUP), 1) // HEAD_DIM
    q8 = q_ref[0].astype(F32)[0:QR, :]
    qbd = jnp.concatenate([jnp.where(lane_h == h, q8, 0.0) for h in range(N_HEADS)], axis=0).astype(BF16)
    dq = dq_ref[0, 0:QR, :]
    dcol = jnp.concatenate([dq[:, h:h + 1] for h in range(N_HEADS)], axis=0)

    def online(s, v_list, v_dims):
        m_prev = m_scr[:, 0:1]
        m_new = jnp.maximum(m_prev, jnp.max(s, axis=-1, keepdims=True))
        alpha = jnp.exp(m_prev - m_new)
        p = jnp.exp(s - m_new)
        l_new = alpha * l_scr[:, 0:1] + jnp.sum(p, axis=-1, keepdims=True)
        pb = p.astype(BF16)
        pv = None
        for n, vp in enumerate(v_list):
            d = lax.dot_general(pb[:, n * PAGE:(n + 1) * PAGE], vp, v_dims, preferred_element_type=F32)
            pv = d if pv is None else pv + d
        acc_scr[...] = acc_scr[...] * alpha + pv
        m_scr[...] = jnp.broadcast_to(m_new, m_scr.shape)
        l_scr[...] = jnp.broadcast_to(l_new, l_scr.shape)

    carry = carry_scr[...]
    s_parts = [None] * PG
    for pi in reversed(range(PG)):
        brow = w_refs[pi][0] + carry
        carry = carry + t_refs[pi][0]
        kp = k_refs[pi][0, 0].astype(BF16)
        s = jnp.dot(qbd, kp, preferred_element_type=F32)
        bias = jnp.concatenate([jnp.broadcast_to(brow[:, h * PAGE:(h + 1) * PAGE], (QR, PAGE))
                                for h in range(N_HEADS)], axis=0)
        s_parts[pi] = s + dcol + bias
    carry_scr[...] = carry
    online(jnp.concatenate(s_parts, axis=1), [v_refs[pi][0, 0].astype(BF16) for pi in range(PG)], _NT)

    @pl.when(g == ng - 1)
    def _():
        s = lax.dot_general(qbd, kn_ref[0], _NT, preferred_element_type=F32)
        dnt = dnt_ref[0]
        bias = jnp.concatenate([jnp.broadcast_to(dnt[h:h + 1, :], (QR, PAGE)) for h in range(N_HEADS)], axis=0)
        s = s + dcol - bias
        t_row = lax.broadcasted_iota(jnp.int32, (N_HEADS * QR, PAGE), 0) % QR
        key = lax.broadcasted_iota(jnp.int32, (N_HEADS * QR, PAGE), 1)
        s = jnp.where((key <= t_row) & (key < T_VALID), s, -jnp.inf)
        online(s, [vn_ref[0]], (((1,), (0,)), ((), ())))
        out = acc_scr[...] / l_scr[:, 0:1]
        yc = jnp.zeros((QR, D_GROUP), F32)
        for h in range(N_HEADS):
            yc = jnp.where(lane_h == h, out[h * QR:(h + 1) * QR, :], yc)
        pad = jnp.zeros((o_ref.shape[1] - QR, D_GROUP), F32)
        o_ref[0] = jnp.concatenate([yc, pad], axis=0).astype(o_ref.dtype)


def _paged_call(page_table, fq, dq, k_new, v_new, dnt, cache_k, cache_v, within, tot, layer, *, PG):
    B, TS, _ = fq.shape
    n_pages = page_table.shape[1]
    assert n_pages % PG == 0
    G = n_pages // PG

    def page_idx(b, g, pt, pi):
        return pt[b, (G - 1 - g) * PG + pi]

    def kv_spec(pi):
        return pl.BlockSpec((1, 1, D_GROUP, PAGE), lambda b, g, pt, _pi=pi: (layer, page_idx(b, g, pt, _pi), 0, 0))

    def w_spec(pi):
        return pl.BlockSpec((1, 1, N_HEADS * PAGE), lambda b, g, pt, _pi=pi: (page_idx(b, g, pt, _pi), 0, 0))

    def per_b(shape):
        return pl.BlockSpec((1,) + shape, lambda b, g, pt: (b,) + (0,) * len(shape))

    in_specs = [per_b((TS, D_GROUP)), per_b((TS, LANE)), per_b((PAGE, D_GROUP)), per_b((PAGE, D_GROUP)),
                per_b((8, PAGE))]
    in_specs += [kv_spec(pi) for pi in range(PG)] + [kv_spec(pi) for pi in range(PG)]
    in_specs += [w_spec(pi) for pi in range(PG)] + [w_spec(pi) for pi in range(PG)]
    grid_spec = pltpu.PrefetchScalarGridSpec(
        num_scalar_prefetch=1,
        grid=(B, G),
        in_specs=in_specs,
        out_specs=per_b((TS, D_GROUP)),
        scratch_shapes=[pltpu.VMEM((N_HEADS * 8, LANE), F32), pltpu.VMEM((N_HEADS * 8, LANE), F32),
                        pltpu.VMEM((N_HEADS * 8, D_GROUP), F32), pltpu.VMEM((1, N_HEADS * PAGE), F32)])
    within3 = within.reshape(within.shape[0], 1, within.shape[1])
    tot3 = tot.reshape(tot.shape[0], 1, tot.shape[1])
    return pl.pallas_call(
        functools.partial(_paged_kernel, PG=PG, T_VALID=4),
        grid_spec=grid_spec,
        out_shape=jax.ShapeDtypeStruct((B, TS, D_GROUP), BF16),
        compiler_params=pltpu.CompilerParams(dimension_semantics=("arbitrary", "arbitrary"),
                                             vmem_limit_bytes=VMEM_LIMIT),
        name="fox_paged",
    )(page_table, fq, dq, k_new, v_new, dnt, *([cache_k] * PG), *([cache_v] * PG), *([within3] * PG),
      *([tot3] * PG))


def _outproj_kernel(yabd_ref, yc_ref, x_ref, wabd_ref, wc_ref, g_ref, b_ref, o_ref, *, alpha):
    m = jnp.dot(yabd_ref[...], wabd_ref[...], preferred_element_type=F32)
    m = m + jnp.dot(yc_ref[...], wc_ref[...], preferred_element_type=F32)
    o_ref[...] = _layer_norm_rows(alpha * x_ref[...] + m, g_ref[...], b_ref[...])


def _outproj_call(yabd, yc, x, wts, *, alpha, TR):
    N, D = x.shape
    assert N % TR == 0

    def rows(w):
        return pl.BlockSpec((TR, w), lambda i: (i, 0))

    def full(a):
        return pl.BlockSpec(a.shape, lambda i: (0, 0))

    params = [wts["w_o_abd"], wts["w_o_c"], wts["ln1_g"], wts["ln1_b"]]
    return pl.pallas_call(
        functools.partial(_outproj_kernel, alpha=alpha),
        grid=(N // TR,),
        in_specs=[rows(3 * D_GROUP), rows(D_GROUP), rows(D)] + [full(a) for a in params],
        out_specs=rows(D),
        out_shape=jax.ShapeDtypeStruct((N, D), F32),
        compiler_params=pltpu.CompilerParams(dimension_semantics=("arbitrary",), vmem_limit_bytes=VMEM_LIMIT),
        name="out_proj_ln",
    )(yabd, yc, x, *params)


def _ffn_kernel(x_ref, wup_ref, cw_ref, cb_ref, wdn_ref, g_ref, b_ref, s0_ref, o_ref, st_ref, xx_scr,
                *, BB, TB, FC, D_FF, T_VALID_LAST, alpha):
    j = pl.program_id(1)
    R = BB * TB

    @pl.when(j == 0)
    def _():
        xx_scr[:, 0:FFN_PAD, :] = s0_ref[...]

    x = x_ref[...].reshape(R, x_ref.shape[-1])
    xb = x.astype(BF16)
    acc = jnp.zeros((R, x.shape[-1]), F32)
    for c in range(D_FF // FC):
        cs = slice(c * FC, (c + 1) * FC)
        gate = jnp.dot(xb, wup_ref[:, cs], preferred_element_type=F32)
        val = jnp.dot(xb, wup_ref[:, D_FF + c * FC:D_FF + (c + 1) * FC], preferred_element_type=F32)
        gate3 = gate.reshape(BB, TB, FC)
        xx_scr[:, FFN_PAD:FFN_PAD + TB, cs] = gate3
        gc = (xx_scr[:, FFN_PAD - 2:FFN_PAD - 2 + TB, cs] * cw_ref[0:1, cs]
              + xx_scr[:, FFN_PAD - 1:FFN_PAD - 1 + TB, cs] * cw_ref[1:2, cs]
              + gate3 * cw_ref[2:3, cs] + cb_ref[:, cs])
        act = (_silu(gc) * val.reshape(BB, TB, FC)).reshape(R, FC).astype(BF16)
        acc = acc + jnp.dot(act, wdn_ref[cs, :], preferred_element_type=F32)
    st_ref[...] = xx_scr[:, T_VALID_LAST + FFN_PAD - 2:T_VALID_LAST + FFN_PAD, :]
    xx_scr[:, 0:FFN_PAD, :] = xx_scr[:, TB:TB + FFN_PAD, :]
    y = _layer_norm_rows(alpha * x + acc, g_ref[...], b_ref[...])
    o_ref[...] = y.reshape(BB, TB, x.shape[-1])


def _ffn_call(x, wts, s0, *, alpha, BB, TB, t_valid_last):
    B, T, D = x.shape
    d_ff = wts["ffn_cw"].shape[1]
    FC = 896 if d_ff % 896 == 0 else d_ff
    params = [wts["w_up"], wts["ffn_cw"], wts["ffn_cb"], wts["w_dn"], wts["ln2_g"], wts["ln2_b"]]

    def full(a):
        return pl.BlockSpec(a.shape, lambda i, j: (0, 0))

    return pl.pallas_call(
        functools.partial(_ffn_kernel, BB=BB, TB=TB, FC=FC, D_FF=d_ff, T_VALID_LAST=t_valid_last, alpha=alpha),
        grid=(B // BB, T // TB),
        in_specs=[pl.BlockSpec((BB, TB, D), lambda i, j: (i, j, 0))] + [full(a) for a in params] +
                 [pl.BlockSpec((BB, FFN_PAD, d_ff), lambda i, j: (i, 0, 0))],
        out_specs=[pl.BlockSpec((BB, TB, D), lambda i, j: (i, j, 0)),
                   pl.BlockSpec((BB, FFN_CONV_WIDTH - 1, d_ff), lambda i, j: (i, 0, 0))],
        out_shape=[jax.ShapeDtypeStruct((B, T, D), F32),
                   jax.ShapeDtypeStruct((B, FFN_CONV_WIDTH - 1, d_ff), F32)],
        scratch_shapes=[pltpu.VMEM((BB, FFN_PAD + TB, d_ff), F32)],
        compiler_params=pltpu.CompilerParams(dimension_semantics=("arbitrary", "arbitrary"),
                                             vmem_limit_bytes=VMEM_LIMIT),
        name="conv_ffn_ln",
    )(x, *params, s0)


def _row(v, width=None):
    v = v.reshape(1, -1).astype(F32)
    if width is not None and v.shape[1] < width:
        v = jnp.pad(v, ((0, 0), (0, width - v.shape[1])))
    return v


def _sgu_mats(sgu_w, sgu_b, chunk, rows):
    w = jnp.tril(sgu_w[:, :chunk, :chunk])
    w = jnp.pad(w, ((0, 0), (0, rows - chunk), (0, rows - chunk)))
    wcat = w.transpose(1, 0, 2).reshape(rows, N_HEADS * rows).astype(BF16)
    b = jnp.pad(sgu_b[:, :chunk], ((0, 0), (0, rows - chunk)))
    bias = jnp.repeat(b.T, HEAD_DIM, axis=1).astype(F32)
    return wcat, bias


def _layer_weights(l, p, sgu_chunks):
    w_o = p["w_o"][l]
    wts = dict(
        w_in=_pack_w_in(p["w_in"][l]),
        w_a=jnp.pad(p["gla_w_a"][l], ((0, LANE - GLA_RANK), (0, 0))).astype(BF16),
        b_a=_row(p["gla_b_a"][l]), gn=_row(p["gla_norm_g"][l]),
        sgu_g=_row(p["sgu_ln_g"][l]), sgu_b=_row(p["sgu_ln_b"][l]),
        bf=_row(p["fox_b_f"][l], LANE),
        cw=jnp.pad(p["conv_w"][l], ((0, CONV_PAD - CONV_WIDTH), (0, 0))).astype(F32),
        cb=_row(p["conv_b"][l]), cng=_row(p["conv_norm_g"][l]), cnb=_row(p["conv_norm_b"][l]),
        w_o_abd=jnp.concatenate([w_o[0:2 * D_GROUP], w_o[3 * D_GROUP:]], axis=0).astype(BF16),
        w_o_c=w_o[2 * D_GROUP:3 * D_GROUP].astype(BF16),
        ln1_g=_row(p["ln1_g"][l]), ln1_b=_row(p["ln1_b"][l]),
        w_up=p["ffn_w_up"][l].astype(BF16),
        ffn_cw=jnp.pad(p["ffn_conv_w"][l], ((0, 8 - FFN_CONV_WIDTH), (0, 0))).astype(F32),
        ffn_cb=_row(p["ffn_conv_b"][l]),
        w_dn=p["ffn_w_down"][l].astype(BF16),
        ln2_g=_row(p["ln2_g"][l]), ln2_b=_row(p["ln2_b"][l]),
        wcat={}, sbias={})
    for rows, chunk in sgu_chunks:
        wts["wcat"][rows], wts["sbias"][rows] = _sgu_mats(p["sgu_w"][l], p["sgu_b"][l], chunk, rows)
    return wts


def _gla_state_in(s):
    B = s.shape[0]
    eye = jnp.eye(N_HEADS, dtype=s.dtype)
    st = s.transpose(0, 1, 3, 2)[:, :, :, None, :] * eye[None, :, None, :, None]
    return st.reshape(B, D_GROUP, GLA_LANES)


def _gla_state_out(st):
    B = st.shape[0]
    s5 = st.reshape(B, N_HEADS, HEAD_DIM, N_HEADS, GLA_DK)
    diag = jnp.stack([s5[:, h, :, h, :] for h in range(N_HEADS)], axis=1)
    return diag.transpose(0, 1, 3, 2)


def kernel(x_prompt, x_sample, cache_fox_k, cache_fox_v, cache_fox_logf, state_gla, state_conv, state_ffn_conv,
           page_table, w_in, gla_w_a, gla_b_a, gla_norm_g, sgu_ln_g, sgu_ln_b, sgu_w, sgu_b, fox_b_f, conv_w,
           conv_b, conv_norm_g, conv_norm_b, w_o, ln1_g, ln1_b, ffn_w_up, ffn_conv_w, ffn_conv_b, ffn_w_down,
           ln2_g, ln2_b):
    p = dict(w_in=w_in, gla_w_a=gla_w_a, gla_b_a=gla_b_a, gla_norm_g=gla_norm_g, sgu_ln_g=sgu_ln_g,
             sgu_ln_b=sgu_ln_b, sgu_w=sgu_w, sgu_b=sgu_b, fox_b_f=fox_b_f, conv_w=conv_w, conv_b=conv_b,
             conv_norm_g=conv_norm_g, conv_norm_b=conv_norm_b, w_o=w_o, ln1_g=ln1_g, ln1_b=ln1_b,
             ffn_w_up=ffn_w_up, ffn_conv_w=ffn_conv_w, ffn_conv_b=ffn_conv_b, ffn_w_down=ffn_w_down,
             ln2_g=ln2_g, ln2_b=ln2_b)
    depth = w_in.shape[0]
    alpha = float((2 * depth) ** 0.25)
    B, T, D = x_prompt.shape
    SB, S, _ = x_sample.shape
    d_ff = ffn_conv_w.shape[-1]
    n_pool = cache_fox_k.shape[1]
    TB = 256
    LG, LS = 64, 128
    SR = SAMPLE_ROWS
    PG = 8

    xp = x_prompt
    xs = jnp.pad(x_sample, ((0, 0), (0, SR - S), (0, 0)))
    ck = cache_fox_k.transpose(0, 1, 3, 4, 2).reshape(depth, n_pool, D_GROUP, PAGE)
    cv = cache_fox_v.transpose(0, 1, 3, 4, 2).reshape(depth, n_pool, D_GROUP, PAGE)
    zeros_gla = jnp.zeros((B, D_GROUP, GLA_LANES), F32)
    zeros_conv = jnp.zeros((B, CONV_PAD, D_GROUP), F32)
    zeros_ffn = jnp.zeros((B, FFN_PAD, d_ff), F32)
    st_p, st_s = [], []
    for l in range(depth):
        wts = _layer_weights(l, p, ((LS, LS), (SR, S)))

        (y, fq, fk, fv, fkb, fvb, logf, dq, dkt, gla_st, conv_st) = _mixer_call(
            xp, wts, zeros_gla, zeros_conv, BB=1, TB=TB, LG=LG, LS=LS, t_valid_last=TB, want_vln=False, kv_t=True)
        yc = _flash_call(fq, fkb, fvb, dq, dkt, TQ=TB)
        x1 = _outproj_call(y.reshape(B * T, -1), yc.reshape(B * T, -1), xp.reshape(B * T, D), wts,
                           alpha=alpha, TR=512).reshape(B, T, D)
        xp, ffn_st = _ffn_call(x1, wts, zeros_ffn, alpha=alpha, BB=1, TB=TB, t_valid_last=TB)
        st_p.append((fk.reshape(B, N_HEADS, HEAD_DIM, T).transpose(0, 3, 1, 2),
                     fv.reshape(B, N_HEADS, HEAD_DIM, T).transpose(0, 3, 1, 2), logf,
                     _gla_state_out(gla_st), conv_st, ffn_st))

        s0_gla = _gla_state_in(state_gla[l])
        s0_conv = jnp.pad(state_conv[l], ((0, 0), (CONV_PAD - (CONV_WIDTH - 1), 0), (0, 0)))
        s0_ffn = jnp.pad(state_ffn_conv[l], ((0, 0), (FFN_PAD - (FFN_CONV_WIDTH - 1), 0), (0, 0)))
        (y, fq, fk, fv, fkb, fvb, logf, dq, dkt, gla_st, conv_st, vln) = _mixer_call(
            xs, wts, s0_gla, s0_conv, BB=SB, TB=SR, LG=SR, LS=SR, t_valid_last=S, want_vln=True, kv_t=False)
        within, tot = _logf_prep_call(cache_fox_logf[l].transpose(0, 2, 1).reshape(n_pool, N_HEADS * PAGE))
        k_new = jnp.pad(fkb, ((0, 0), (0, PAGE - SR), (0, 0)))
        v_new = jnp.pad(fvb, ((0, 0), (0, PAGE - SR), (0, 0)))
        dnt = jnp.pad(dkt[:, 0], ((0, 0), (0, 0), (0, PAGE - SR)))
        yc = _paged_call(page_table, fq, dq, k_new, v_new, dnt, ck, cv, within, tot, l, PG=PG)
        x1 = _outproj_call(y.reshape(SB * SR, -1), yc.reshape(SB * SR, -1), xs.reshape(SB * SR, D), wts,
                           alpha=alpha, TR=SB * SR).reshape(SB, SR, D)
        xs, ffn_st = _ffn_call(x1, wts, s0_ffn, alpha=alpha, BB=SB, TB=SR, t_valid_last=S)
        st_s.append((fk[:, :S].reshape(SB, S, N_HEADS, HEAD_DIM), fv[:, :S].reshape(SB, S, N_HEADS, HEAD_DIM),
                     logf[:, :S], _gla_state_out(gla_st), conv_st, ffn_st, vln[:, :S]))

    def stack(states, i):
        return jnp.stack([s[i] for s in states])

    return (xp, xs[:, :S], stack(st_p, 0), stack(st_p, 1), stack(st_p, 2), stack(st_p, 3), stack(st_p, 4),
            stack(st_p, 5), stack(st_s, 0), stack(st_s, 1), stack(st_s, 2), stack(st_s, 3), stack(st_s, 4),
            stack(st_s, 5), stack(st_s, 6))
```

```python
import functools

import numpy as np
import jax
import jax.numpy as jnp
from jax import lax
from jax.experimental import pallas as pl
from jax.experimental.pallas import tpu as pltpu

F32 = jnp.float32
BF16 = jnp.bfloat16

N_HEADS = 4
HEAD_DIM = 64
D_GROUP = N_HEADS * HEAD_DIM
GLA_DK = 32
GLA_LANES = N_HEADS * GLA_DK
GLA_RANK = 16
GLA_TAU = 16.0
CONV_WIDTH = 31
CONV_PAD = 32
FFN_CONV_WIDTH = 3
FFN_PAD = 8
EPS = 1e-5
LANE = 128
PAGE = 128
SAMPLE_ROWS = 16
VMEM_LIMIT = 56 * 1024 * 1024

_SEGS = (("aq", 128, 128), ("ak", 128, 128), ("av", 256, 256), ("ag", 256, 256), ("alr", 16, 128),
         ("bu", 256, 256), ("bv", 256, 256), ("cq", 256, 256), ("ck", 256, 256), ("cv", 256, 256),
         ("cf", 4, 128), ("d1", 256, 256), ("d2", 256, 256))
_OFF = {}
_o = 0
for _n, _w, _p in _SEGS:
    _OFF[_n] = _o
    _o += _p
D_IN_PAD = _o


def _cols(name, width):
    return slice(_OFF[name], _OFF[name] + width)


def _pack_w_in(w_in):
    parts, src = [], 0
    for _, w, p in _SEGS:
        blk = w_in[:, src:src + w]
        if p > w:
            blk = jnp.pad(blk, ((0, 0), (0, p - w)))
        parts.append(blk)
        src += w
    return jnp.concatenate(parts, axis=1).astype(BF16)


def _split_dot_rhs(x, m, passes=3):
    acc, r = None, x
    for p in range(passes):
        hi = r.astype(BF16)
        d = jnp.dot(hi, m, preferred_element_type=F32)
        acc = d if acc is None else acc + d
        if p + 1 < passes:
            r = r - hi.astype(F32)
    return acc


def _split_dot_lhs(m, x, passes=3):
    acc, r = None, x
    for p in range(passes):
        hi = r.astype(BF16)
        d = jnp.dot(m, hi, preferred_element_type=F32)
        acc = d if acc is None else acc + d
        if p + 1 < passes:
            r = r - hi.astype(F32)
    return acc


_NT = (((1,), (1,)), ((), ()))
_TN = (((0,), (0,)), ((), ()))


def _sigmoid(x):
    return 1.0 / (1.0 + jnp.exp(-x))


def _silu(x):
    return x * _sigmoid(x)


def _log_sigmoid(x):
    return jnp.minimum(x, 0.0) - jnp.log1p(jnp.exp(-jnp.abs(x)))


def _group_norm(x, ones_bd, g, b):
    inv = 1.0 / HEAD_DIM
    mu = _split_dot_rhs(x, ones_bd, passes=2) * inv
    xc = x - mu
    var = _split_dot_rhs(xc * xc, ones_bd, passes=2) * inv
    return xc * lax.rsqrt(var + EPS) * g + b


def _layer_norm_rows(y, g, b):
    mu = jnp.mean(y, axis=-1, keepdims=True)
    yc = y - mu
    var = jnp.mean(yc * yc, axis=-1, keepdims=True)
    return yc * lax.rsqrt(var + EPS) * g + b


def _gla_consts(L):
    K = L.bit_length() - 1
    assert 1 << K == L
    t = np.arange(L)
    rows = []
    for l in range(K + 1):
        m = (t >> l) << l
        rows.append((t[None, :] >= m[:, None]) & (t[None, :] <= t[:, None]))
    for l in range(K + 1):
        mp = ((t >> l) + 1) << l
        rows.append((t[None, :] >= t[:, None] + 1) & (t[None, :] <= mp[:, None] - 1))
    big = np.concatenate(rows, 0).astype(np.float32)
    x = t[:, None] ^ t[None, :]
    lev = np.where(x > 0, np.floor(np.log2(np.maximum(x, 1))).astype(np.int32), -1)
    lev = np.where(t[None, :] > t[:, None], -2, lev).astype(np.int32)
    return big, np.tile(lev, (1, N_HEADS))


def _head_mask(rows_per_head, lanes_per_head):
    r = np.arange(N_HEADS * rows_per_head)[:, None] // rows_per_head
    c = np.arange(N_HEADS * lanes_per_head)[None, :] // lanes_per_head
    return (r == c).astype(np.float32)


def _mixer_kernel(x_ref, w_in_ref, w_a_ref, b_a_ref, gn_ref, sgu_g_ref, sgu_b_ref, wcat_ref, sbias_ref,
                  bf_ref, cw_ref, cb_ref, cng_ref, cnb_ref,
                  big_ref, lev_ref, hmk_ref, hmv_ref, hms_ref, sbd_ref, ones_ref, tril_ref,
                  s0_gla_ref, s0_conv_ref,
                  y_ref, fq_ref, fk_ref, fv_ref, fkb_ref, fvb_ref, logf_ref, dq_ref, dkt_ref,
                  gla_out_ref, conv_out_ref, vln_ref,
                  h_scr, gla_scr, xx_scr, dc_scr, xph_scr,
                  *, BB, TB, LG, LS, T_VALID_LAST, KLEV, KV_T):
    j = pl.program_id(1)
    nj = pl.num_programs(1)
    R = BB * TB

    @pl.when(j == 0)
    def _():
        gla_scr[...] = s0_gla_ref[...]
        xx_scr[:, 0:CONV_PAD, :] = s0_conv_ref[...]
        dc_scr[...] = jnp.zeros_like(dc_scr)

    xb = x_ref[...].reshape(R, x_ref.shape[-1]).astype(BF16)
    h_scr[...] = jnp.dot(xb, w_in_ref[...], preferred_element_type=F32)

    ones_bd = ones_ref[...]
    lev = lev_ref[...]
    hmk = hmk_ref[...]
    hmv = hmv_ref[...]
    hms = hms_ref[...]
    sbd = sbd_ref[...]
    big = big_ref[...]
    row_t = lax.broadcasted_iota(jnp.int32, (TB, 1), 0)
    valid = row_t < T_VALID_LAST

    def seq_body(bb, carry):
        r0 = 0 if BB == 1 else pl.multiple_of(bb * TB, TB)

        def hs(c0, n, name, width):
            return h_scr[pl.ds(r0 + c0, n), _cols(name, width)]

        S = gla_scr[bb]
        for c in range(TB // LG):
            c0 = c * LG
            q = hs(c0, LG, "aq", 128) * (GLA_DK ** -0.5)
            k = hs(c0, LG, "ak", 128)
            v = hs(c0, LG, "av", 256)
            lr = hs(c0, LG, "alr", 128).astype(BF16)
            xa = jnp.dot(lr, w_a_ref[...], preferred_element_type=F32) + b_a_ref[...]
            la = _log_sigmoid(xa) * (1.0 / GLA_TAU)
            if T_VALID_LAST < TB:
                vm = valid[c0:c0 + LG]
                la = jnp.where(vm, la, 0.0)
                k = jnp.where(vm, k, 0.0)
            X = jnp.exp(_split_dot_lhs(big, la))

            def qs(l):
                return X[l * LG:(l + 1) * LG]

            def ks(l):
                return X[(KLEV + 1 + l) * LG:(KLEV + 2 + l) * LG]

            def kstack(kk):
                kb = kk.astype(BF16)
                return jnp.concatenate([kb] * N_HEADS, axis=0) * hmk

            Rl = lax.dot_general(q.astype(BF16), kstack(k), _NT, preferred_element_type=F32)
            attn = jnp.where(lev == -1, Rl, 0.0)
            for l in range(KLEV):
                Rl = lax.dot_general((q * qs(l)).astype(BF16), kstack(k * ks(l)), _NT,
                                     preferred_element_type=F32)
                attn = jnp.where(lev == l, Rl, attn)
            vb = v.astype(BF16)
            vbd = jnp.concatenate([vb] * N_HEADS, axis=0) * hmv
            o = jnp.dot(attn.astype(BF16), vbd, preferred_element_type=F32)
            o = o + lax.dot_general((q * qs(KLEV)).astype(BF16), S.astype(BF16), _NT,
                                    preferred_element_type=F32)
            khat = (k * ks(KLEV)).astype(BF16)
            upd = lax.dot_general(vb, khat, _TN, preferred_element_type=F32)
            decay = X[(KLEV + 1) * LG - 1:(KLEV + 1) * LG]
            S = S * decay + upd * sbd
            ms = _split_dot_rhs(o * o, ones_bd, passes=2) * (1.0 / HEAD_DIM)
            o = o * lax.rsqrt(ms + EPS) * gn_ref[...]
            y_a = o * _silu(hs(c0, LG, "ag", 256))
            y_ref[bb, c0:c0 + LG, 0:256] = y_a.astype(y_ref.dtype)
        gla_scr[bb] = S

        for c in range(TB // LS):
            c0 = c * LS
            v_ln = _group_norm(hs(c0, LS, "bv", 256), ones_bd, sgu_g_ref[...], sgu_b_ref[...])
            if vln_ref is not None:
                vln_ref[bb, c0:c0 + LS, :] = v_ln
            vbd = jnp.concatenate([v_ln.astype(BF16)] * N_HEADS, axis=0) * hms
            mixed = jnp.dot(wcat_ref[...], vbd, preferred_element_type=F32) + sbias_ref[...]
            y_b = hs(c0, LS, "bu", 256) * mixed
            y_ref[bb, c0:c0 + LS, 256:512] = y_b.astype(y_ref.dtype)

        glu = hs(0, TB, "d1", 256) * _sigmoid(hs(0, TB, "d2", 256))
        xx_scr[bb, CONV_PAD:CONV_PAD + TB, :] = glu
        NPH = CONV_PAD + TB - 8
        for r in range(1, 8):
            xph_scr[r, 0:NPH, :] = xx_scr[bb, r:r + NPH, :]
        CR = min(TB, 64)
        for c in range(TB // CR):
            c0 = c * CR
            acc = jnp.zeros((CR, D_GROUP), F32) + cb_ref[...]
            for tap in range(CONV_WIDTH):
                s = c0 + CONV_PAD - (CONV_WIDTH - 1) + tap
                a8, r = (s // 8) * 8, s % 8
                rows = xx_scr[bb, a8:a8 + CR, :] if r == 0 else xph_scr[r, a8:a8 + CR, :]
                acc = acc + rows * cw_ref[tap:tap + 1, :]
            y_d = _silu(_group_norm(acc, ones_bd, cng_ref[...], cnb_ref[...]))
            y_ref[bb, c0:c0 + CR, 512:768] = y_d.astype(y_ref.dtype)
        tail0 = T_VALID_LAST + CONV_PAD - (CONV_WIDTH - 1)
        conv_out_ref[bb] = xx_scr[bb, tail0:tail0 + CONV_WIDTH - 1, :]
        xx_scr[bb, 0:CONV_PAD, :] = xx_scr[bb, TB:TB + CONV_PAD, :]

        fq_ref[bb] = (hs(0, TB, "cq", 256) * (HEAD_DIM ** -0.5)).astype(BF16)
        fk = hs(0, TB, "ck", 256)
        fv = hs(0, TB, "cv", 256)
        fk_ref[bb] = fk.T if KV_T else fk
        fv_ref[bb] = fv.T if KV_T else fv
        fkb_ref[bb] = fk.astype(BF16)
        fvb_ref[bb] = fv.astype(BF16)
        logf = _log_sigmoid(hs(0, TB, "cf", 128) + bf_ref[...])
        logf_ref[bb] = logf[:, 0:N_HEADS]
        dcum = _split_dot_lhs(tril_ref[...], logf) + dc_scr[bb, 0:1, :]
        dq_ref[bb] = dcum
        dkt_ref[bb, 0] = dcum.T[0:8, :]
        dc_scr[bb] = jnp.broadcast_to(dcum[TB - 1:TB, :], (8, LANE))
        return carry

    if BB == 1:
        seq_body(0, 0)
    else:
        lax.fori_loop(0, BB, seq_body, 0)
    gla_out_ref[...] = gla_scr[...]
    del nj


def _mixer_call(x, wts, s0_gla, s0_conv, *, BB, TB, LG, LS, t_valid_last, want_vln, kv_t):
    B, T, D = x.shape
    assert B % BB == 0 and T % TB == 0 and TB % LG == 0 and TB % LS == 0
    klev = LG.bit_length() - 1
    big, lev = _gla_consts(LG)
    consts = [jnp.asarray(big, BF16), jnp.asarray(lev), jnp.asarray(_head_mask(LG, GLA_DK), BF16),
              jnp.asarray(_head_mask(LG, HEAD_DIM), BF16), jnp.asarray(_head_mask(LS, HEAD_DIM), BF16),
              jnp.asarray(_head_mask(HEAD_DIM, GLA_DK), F32), jnp.asarray(_head_mask(HEAD_DIM, HEAD_DIM), BF16),
              jnp.asarray(np.tril(np.ones((TB, TB), np.float32)), BF16)]
    params = [wts["w_in"], wts["w_a"], wts["b_a"], wts["gn"], wts["sgu_g"], wts["sgu_b"],
              wts["wcat"][LS], wts["sbias"][LS], wts["bf"], wts["cw"], wts["cb"], wts["cng"], wts["cnb"]]

    def full(a):
        return pl.BlockSpec(a.shape, lambda i, j, _n=a.ndim: (0,) * _n)

    def seq3(width):
        return pl.BlockSpec((BB, TB, width), lambda i, j: (i, j, 0))

    def per_b(shape):
        return pl.BlockSpec((BB,) + shape, lambda i, j: (i,) + (0,) * len(shape))

    in_specs = [seq3(D)] + [full(a) for a in params] + [full(a) for a in consts] + \
               [per_b(s0_gla.shape[1:]), per_b(s0_conv.shape[1:])]
    out_shape = [jax.ShapeDtypeStruct((B, T, 3 * D_GROUP), BF16),
                 jax.ShapeDtypeStruct((B, T, D_GROUP), BF16),
                 jax.ShapeDtypeStruct((B, D_GROUP, T) if kv_t else (B, T, D_GROUP), F32),
                 jax.ShapeDtypeStruct((B, D_GROUP, T) if kv_t else (B, T, D_GROUP), F32),
                 jax.ShapeDtypeStruct((B, T, D_GROUP), BF16),
                 jax.ShapeDtypeStruct((B, T, D_GROUP), BF16),
                 jax.ShapeDtypeStruct((B, T, N_HEADS), F32),
                 jax.ShapeDtypeStruct((B, T, LANE), F32),
                 jax.ShapeDtypeStruct((B, T // TB, 8, TB), F32),
                 jax.ShapeDtypeStruct((B, D_GROUP, GLA_LANES), F32),
                 jax.ShapeDtypeStruct((B, CONV_WIDTH - 1, D_GROUP), F32)]
    kv_spec = pl.BlockSpec((BB, D_GROUP, TB), lambda i, j: (i, 0, j)) if kv_t else seq3(D_GROUP)
    out_specs = [seq3(3 * D_GROUP), seq3(D_GROUP), kv_spec, kv_spec, seq3(D_GROUP), seq3(D_GROUP),
                 seq3(N_HEADS), seq3(LANE),
                 pl.BlockSpec((BB, 1, 8, TB), lambda i, j: (i, j, 0, 0)),
                 per_b((D_GROUP, GLA_LANES)), per_b((CONV_WIDTH - 1, D_GROUP))]
    if want_vln:
        out_shape.append(jax.ShapeDtypeStruct((B, T, D_GROUP), F32))
        out_specs.append(seq3(D_GROUP))

    def body(*refs):
        n_in = len(in_specs)
        n_out = len(out_specs)
        ins, outs, scr = refs[:n_in], list(refs[n_in:n_in + n_out]), refs[n_in + n_out:]
        if not want_vln:
            outs.append(None)
        _mixer_kernel(*ins, *outs, *scr, BB=BB, TB=TB, LG=LG, LS=LS, T_VALID_LAST=t_valid_last, KLEV=klev,
                      KV_T=kv_t)

    return pl.pallas_call(
        body,
        grid=(B // BB, T // TB),
        in_specs=in_specs,
        out_specs=out_specs,
        out_shape=out_shape,
        scratch_shapes=[pltpu.VMEM((BB * TB, D_IN_PAD), F32),
                        pltpu.VMEM((BB, D_GROUP, GLA_LANES), F32),
                        pltpu.VMEM((BB, CONV_PAD + TB, D_GROUP), F32),
                        pltpu.VMEM((BB, 8, LANE), F32),
                        pltpu.VMEM((8, CONV_PAD + TB, D_GROUP), F32)],
        compiler_params=pltpu.CompilerParams(dimension_semantics=("arbitrary", "arbitrary"),
                                             vmem_limit_bytes=VMEM_LIMIT),
        name="mixer_pre",
    )(x, *params, *consts, s0_gla, s0_conv)


FLASH_STRIP = 64


def _flash_kernel(q_ref, k_ref, v_ref, dq_ref, dkt_ref, o_ref,
                  qbd_scr, dqb_scr, s_scr, p_scr, m_scr, l_scr, a_scr, acc_scr, *, TQ):
    assert TQ == 2 * LANE
    i = pl.program_id(1)
    RS = FLASH_STRIP
    NS = TQ // RS
    q = q_ref[0]
    lane_h = lax.broadcasted_iota(jnp.int32, (1, D_GROUP), 1) // HEAD_DIM
    dq = dq_ref[0]
    for h in range(N_HEADS):
        qbd_scr[h * TQ:(h + 1) * TQ, :] = jnp.where(lane_h == h, q, jnp.zeros_like(q))
        dqb_scr[h * TQ:(h + 1) * TQ, :] = jnp.broadcast_to(dq[:, h:h + 1], (TQ, LANE))
    m_scr[...] = jnp.full_like(m_scr, -jnp.inf)
    l_scr[...] = jnp.zeros_like(l_scr)
    acc_scr[...] = jnp.zeros_like(acc_scr)

    def step(jb, masked):
        start = pl.multiple_of(jb * TQ, TQ)
        k = k_ref[0, pl.ds(start, TQ), :]
        v = v_ref[0, pl.ds(start, TQ), :]
        dk = dkt_ref[0, jb]
        s_scr[...] = lax.dot_general(qbd_scr[...], k, _NT, preferred_element_type=F32)
        for h in range(N_HEADS):
            dkh = dk[h:h + 1, :]
            for r in range(NS):
                rows = slice(h * TQ + r * RS, h * TQ + (r + 1) * RS)
                s = s_scr[rows, :] - dkh
                if masked:
                    q_pos = r * RS + lax.broadcasted_iota(jnp.int32, (RS, TQ), 0)
                    k_pos = lax.broadcasted_iota(jnp.int32, (RS, TQ), 1)
                    s = jnp.where(k_pos <= q_pos, s, -jnp.inf)
                dqr = dqb_scr[rows, :]
                m_prev = m_scr[rows, :]
                m_new = jnp.maximum(m_prev, dqr + jnp.max(s, axis=-1, keepdims=True))
                c = dqr - m_new
                p = jnp.exp(s + jnp.concatenate([c, c], axis=1))
                alpha = jnp.exp(m_prev - m_new)
                l_scr[rows, :] = alpha * l_scr[rows, :] + jnp.sum(p, axis=-1, keepdims=True)
                m_scr[rows, :] = m_new
                a_scr[rows, :] = alpha
                p_scr[rows, :] = p.astype(BF16)
        s_scr[...] = jnp.dot(p_scr[...], v, preferred_element_type=F32)
        for r in range(NS):
            pv = jnp.zeros((RS, D_GROUP), F32)
            sc = jnp.zeros((RS, D_GROUP), F32)
            for h in range(N_HEADS):
                rows = slice(h * TQ + r * RS, h * TQ + (r + 1) * RS)
                a = a_scr[rows, :]
                pv = jnp.where(lane_h == h, s_scr[rows, :], pv)
                sc = jnp.where(lane_h == h, jnp.concatenate([a, a], axis=1), sc)
            acc_scr[r * RS:(r + 1) * RS, :] = acc_scr[r * RS:(r + 1) * RS, :] * sc + pv

    def body(jb, carry):
        step(jb, False)
        return carry

    lax.fori_loop(0, i, body, 0)
    step(i, True)
    for r in range(NS):
        den = jnp.zeros((RS, D_GROUP), F32)
        for h in range(N_HEADS):
            lh = l_scr[h * TQ + r * RS:h * TQ + (r + 1) * RS, :]
            den = jnp.where(lane_h == h, jnp.concatenate([lh, lh], axis=1), den)
        o_ref[0, r * RS:(r + 1) * RS, :] = (acc_scr[r * RS:(r + 1) * RS, :] / den).astype(o_ref.dtype)


def _flash_call(fq, fkb, fvb, dq, dkt, *, TQ):
    B, T, _ = fq.shape
    HQ = N_HEADS * TQ
    return pl.pallas_call(
        functools.partial(_flash_kernel, TQ=TQ),
        grid=(B, T // TQ),
        in_specs=[pl.BlockSpec((1, TQ, D_GROUP), lambda b, i: (b, i, 0)),
                  pl.BlockSpec((1, T, D_GROUP), lambda b, i: (b, 0, 0)),
                  pl.BlockSpec((1, T, D_GROUP), lambda b, i: (b, 0, 0)),
                  pl.BlockSpec((1, TQ, LANE), lambda b, i: (b, i, 0)),
                  pl.BlockSpec((1, T // TQ, 8, TQ), lambda b, i: (b, 0, 0, 0))],
        out_specs=pl.BlockSpec((1, TQ, D_GROUP), lambda b, i: (b, i, 0)),
        out_shape=jax.ShapeDtypeStruct((B, T, D_GROUP), BF16),
        scratch_shapes=[pltpu.VMEM((HQ, D_GROUP), BF16), pltpu.VMEM((HQ, LANE), F32),
                        pltpu.VMEM((HQ, TQ), F32), pltpu.VMEM((HQ, TQ), BF16),
                        pltpu.VMEM((HQ, LANE), F32), pltpu.VMEM((HQ, LANE), F32), pltpu.VMEM((HQ, LANE), F32),
                        pltpu.VMEM((TQ, D_GROUP), F32)],
        compiler_params=pltpu.CompilerParams(dimension_semantics=("arbitrary", "arbitrary"),
                                             vmem_limit_bytes=VMEM_LIMIT),
        name="fox_prompt",
    )(fq, fkb, fvb, dq, dkt)


def _logf_prep_kernel(lp_ref, u_ref, te_ref, within_ref, tot_ref):
    lp = lp_ref[...]
    within_ref[...] = _split_dot_rhs(lp, u_ref[...])
    tot_ref[...] = _split_dot_rhs(lp, te_ref[...])


def _logf_prep_call(lp2d):
    n_pool, W = lp2d.shape
    r = np.arange(W)
    h_in, s_in = r // PAGE, r % PAGE
    h_out, s_out = r // PAGE, r % PAGE
    same_h = h_in[:, None] == h_out[None, :]
    u = jnp.asarray((same_h & (s_in[:, None] > s_out[None, :])).astype(np.float32), BF16)
    te = jnp.asarray(same_h.astype(np.float32), BF16)
    RB = n_pool
    for cand in (640, 512, 256, 128, 64, 32, 16, 8):
        if n_pool % cand == 0:
            RB = cand
            break
    return pl.pallas_call(
        _logf_prep_kernel,
        grid=(n_pool // RB,),
        in_specs=[pl.BlockSpec((RB, W), lambda i: (i, 0)),
                  pl.BlockSpec((W, W), lambda i: (0, 0)),
                  pl.BlockSpec((W, W), lambda i: (0, 0))],
        out_specs=[pl.BlockSpec((RB, W), lambda i: (i, 0)), pl.BlockSpec((RB, W), lambda i: (i, 0))],
        out_shape=[jax.ShapeDtypeStruct((n_pool, W), F32), jax.ShapeDtypeStruct((n_pool, W), F32)],
        compiler_params=pltpu.CompilerParams(dimension_semantics=("arbitrary",), vmem_limit_bytes=VMEM_LIMIT),
        name="logf_suffix",
    )(lp2d, u, te)


def _paged_kernel(pt_ref, q_ref, dq_ref, kn_ref, vn_ref, dnt_ref, within_ref, tot_ref, ck_hbm, cv_hbm, o_ref,
                  kbuf, vbuf, sem, m_scr, l_scr, acc_scr, carry_scr, *, LAYER, PG, T_VALID):
    B, n_pages = pt_ref.shape
    G = n_pages // PG
    n_groups = B * G
    QR = 8
    lane_h = lax.broadcasted_iota(jnp.int32, (1, D_GROUP), 1) // HEAD_DIM

    def group_pages(i):
        return i // G, (G - 1 - i % G) * PG

    def page_copies(i, slot):
        b, p0 = group_pages(i)
        cps = []
        for pi in range(PG):
            pid = pt_ref[b, p0 + pi]
            cps.append(pltpu.make_async_copy(ck_hbm.at[LAYER, pid], kbuf.at[slot, pi], sem.at[0, slot]))
            cps.append(pltpu.make_async_copy(cv_hbm.at[LAYER, pid], vbuf.at[slot, pi], sem.at[1, slot]))
        return cps

    def online(s, v_list, v_dims):
        m_prev = m_scr[:, 0:1]
        m_new = jnp.maximum(m_prev, jnp.max(s, axis=-1, keepdims=True))
        alpha = jnp.exp(m_prev - m_new)
        p = jnp.exp(s - m_new)
        l_new = alpha * l_scr[:, 0:1] + jnp.sum(p, axis=-1, keepdims=True)
        pb = p.astype(BF16)
        pv = None
        for n, vp in enumerate(v_list):
            d = lax.dot_general(pb[:, n * PAGE:(n + 1) * PAGE], vp, v_dims, preferred_element_type=F32)
            pv = d if pv is None else pv + d
        acc_scr[...] = acc_scr[...] * alpha + pv
        m_scr[...] = jnp.broadcast_to(m_new, m_scr.shape)
        l_scr[...] = jnp.broadcast_to(l_new, l_scr.shape)

    def compute(i, slot):
        b, p0 = group_pages(i)
        g = i % G

        @pl.when(g == 0)
        def _():
            m_scr[...] = jnp.full_like(m_scr, -jnp.inf)
            l_scr[...] = jnp.zeros_like(l_scr)
            acc_scr[...] = jnp.zeros_like(acc_scr)
            carry_scr[...] = jnp.zeros_like(carry_scr)

        q8 = q_ref[b].astype(F32)[0:QR, :]
        qbd = jnp.concatenate([jnp.where(lane_h == h, q8, 0.0) for h in range(N_HEADS)], axis=0).astype(BF16)
        dq = dq_ref[b][0:QR, :]
        dcol = jnp.concatenate([dq[:, h:h + 1] for h in range(N_HEADS)], axis=0)

        carry = carry_scr[...]
        s_parts = [None] * PG
        for pi in reversed(range(PG)):
            pid = pt_ref[b, p0 + pi]
            brow = within_ref[pl.ds(pid, 1), :] + carry
            carry = carry + tot_ref[pl.ds(pid, 1), :]
            kp = kbuf[slot, pi].astype(BF16)
            s = jnp.dot(qbd, kp, preferred_element_type=F32)
            bias = jnp.concatenate([jnp.broadcast_to(brow[:, h * PAGE:(h + 1) * PAGE], (QR, PAGE))
                                    for h in range(N_HEADS)], axis=0)
            s_parts[pi] = s + dcol + bias
        carry_scr[...] = carry
        online(jnp.concatenate(s_parts, axis=1), [vbuf[slot, pi].astype(BF16) for pi in range(PG)], _NT)

        @pl.when(g == G - 1)
        def _():
            s = lax.dot_general(qbd, kn_ref[b], _NT, preferred_element_type=F32)
            dnt = dnt_ref[b]
            bias = jnp.concatenate([jnp.broadcast_to(dnt[h:h + 1, :], (QR, PAGE)) for h in range(N_HEADS)],
                                   axis=0)
            s = s + dcol - bias
            t_row = lax.broadcasted_iota(jnp.int32, (N_HEADS * QR, PAGE), 0) % QR
            key = lax.broadcasted_iota(jnp.int32, (N_HEADS * QR, PAGE), 1)
            s = jnp.where((key <= t_row) & (key < T_VALID), s, -jnp.inf)
            online(s, [vn_ref[b]], (((1,), (0,)), ((), ())))
            out = acc_scr[...] / l_scr[:, 0:1]
            yc = jnp.zeros((QR, D_GROUP), F32)
            for h in range(N_HEADS):
                yc = jnp.where(lane_h == h, out[h * QR:(h + 1) * QR, :], yc)
            pad = jnp.zeros((o_ref.shape[1] - QR, D_GROUP), F32)
            o_ref[b] = jnp.concatenate([yc, pad], axis=0).astype(o_ref.dtype)

    for cp in page_copies(0, 0):
        cp.start()

    def body(i, c):
        slot = i % 2

        @pl.when(i + 1 < n_groups)
        def _():
            for cp in page_copies(i + 1, 1 - slot):
                cp.start()

        for cp in page_copies(i, slot):
            cp.wait()
        compute(i, slot)
        return c

    lax.fori_loop(0, n_groups, body, 0)


def _paged_call(page_table, fq, dq, k_new, v_new, dnt, cache_k, cache_v, within, tot, layer, *, PG):
    B, TS, _ = fq.shape
    assert page_table.shape[1] % PG == 0
    vmem = pl.BlockSpec(memory_space=pltpu.VMEM)
    hbm = pl.BlockSpec(memory_space=pl.ANY)
    return pl.pallas_call(
        functools.partial(_paged_kernel, LAYER=layer, PG=PG, T_VALID=4),
        in_specs=[pl.BlockSpec(memory_space=pltpu.SMEM)] + [vmem] * 7 + [hbm, hbm],
        out_specs=vmem,
        out_shape=jax.ShapeDtypeStruct((B, TS, D_GROUP), BF16),
        scratch_shapes=[pltpu.VMEM((2, PG, D_GROUP, PAGE), F32), pltpu.VMEM((2, PG, D_GROUP, PAGE), F32),
                        pltpu.SemaphoreType.DMA((2, 2)),
                        pltpu.VMEM((N_HEADS * 8, LANE), F32), pltpu.VMEM((N_HEADS * 8, LANE), F32),
                        pltpu.VMEM((N_HEADS * 8, D_GROUP), F32), pltpu.VMEM((1, N_HEADS * PAGE), F32)],
        compiler_params=pltpu.CompilerParams(vmem_limit_bytes=VMEM_LIMIT),
        name="fox_paged",
    )(page_table, fq, dq, k_new, v_new, dnt, within, tot, cache_k, cache_v)


def _outproj_kernel(yabd_ref, yc_ref, x_ref, wabd_ref, wc_ref, g_ref, b_ref, o_ref, *, alpha):
    m = jnp.dot(yabd_ref[...], wabd_ref[...], preferred_element_type=F32)
    m = m + jnp.dot(yc_ref[...], wc_ref[...], preferred_element_type=F32)
    o_ref[...] = _layer_norm_rows(alpha * x_ref[...] + m, g_ref[...], b_ref[...])


def _outproj_call(yabd, yc, x, wts, *, alpha, TR):
    N, D = x.shape
    assert N % TR == 0

    def rows(w):
        return pl.BlockSpec((TR, w), lambda i: (i, 0))

    def full(a):
        return pl.BlockSpec(a.shape, lambda i: (0, 0))

    params = [wts["w_o_abd"], wts["w_o_c"], wts["ln1_g"], wts["ln1_b"]]
    return pl.pallas_call(
        functools.partial(_outproj_kernel, alpha=alpha),
        grid=(N // TR,),
        in_specs=[rows(3 * D_GROUP), rows(D_GROUP), rows(D)] + [full(a) for a in params],
        out_specs=rows(D),
        out_shape=jax.ShapeDtypeStruct((N, D), F32),
        compiler_params=pltpu.CompilerParams(dimension_semantics=("arbitrary",), vmem_limit_bytes=VMEM_LIMIT),
        name="out_proj_ln",
    )(yabd, yc, x, *params)


def _ffn_kernel(x_ref, wup_ref, cw_ref, cb_ref, wdn_ref, g_ref, b_ref, s0_ref, o_ref, st_ref, xx_scr,
                *, BB, TB, FC, D_FF, T_VALID_LAST, alpha):
    j = pl.program_id(1)
    R = BB * TB

    @pl.when(j == 0)
    def _():
        xx_scr[:, 0:FFN_PAD, :] = s0_ref[...]

    x = x_ref[...].reshape(R, x_ref.shape[-1])
    xb = x.astype(BF16)
    acc = jnp.zeros((R, x.shape[-1]), F32)
    for c in range(D_FF // FC):
        cs = slice(c * FC, (c + 1) * FC)
        gate = jnp.dot(xb, wup_ref[:, cs], preferred_element_type=F32)
        val = jnp.dot(xb, wup_ref[:, D_FF + c * FC:D_FF + (c + 1) * FC], preferred_element_type=F32)
        gate3 = gate.reshape(BB, TB, FC)
        xx_scr[:, FFN_PAD:FFN_PAD + TB, cs] = gate3
        gc = (xx_scr[:, FFN_PAD - 2:FFN_PAD - 2 + TB, cs] * cw_ref[0:1, cs]
              + xx_scr[:, FFN_PAD - 1:FFN_PAD - 1 + TB, cs] * cw_ref[1:2, cs]
              + gate3 * cw_ref[2:3, cs] + cb_ref[:, cs])
        act = (_silu(gc) * val.reshape(BB, TB, FC)).reshape(R, FC).astype(BF16)
        acc = acc + jnp.dot(act, wdn_ref[cs, :], preferred_element_type=F32)
    st_ref[...] = xx_scr[:, T_VALID_LAST + FFN_PAD - 2:T_VALID_LAST + FFN_PAD, :]
    xx_scr[:, 0:FFN_PAD, :] = xx_scr[:, TB:TB + FFN_PAD, :]
    y = _layer_norm_rows(alpha * x + acc, g_ref[...], b_ref[...])
    o_ref[...] = y.reshape(BB, TB, x.shape[-1])


def _ffn_call(x, wts, s0, *, alpha, BB, TB, t_valid_last):
    B, T, D = x.shape
    d_ff = wts["ffn_cw"].shape[1]
    FC = 896 if d_ff % 896 == 0 else d_ff
    params = [wts["w_up"], wts["ffn_cw"], wts["ffn_cb"], wts["w_dn"], wts["ln2_g"], wts["ln2_b"]]

    def full(a):
        return pl.BlockSpec(a.shape, lambda i, j: (0, 0))

    return pl.pallas_call(
        functools.partial(_ffn_kernel, BB=BB, TB=TB, FC=FC, D_FF=d_ff, T_VALID_LAST=t_valid_last, alpha=alpha),
        grid=(B // BB, T // TB),
        in_specs=[pl.BlockSpec((BB, TB, D), lambda i, j: (i, j, 0))] + [full(a) for a in params] +
                 [pl.BlockSpec((BB, FFN_PAD, d_ff), lambda i, j: (i, 0, 0))],
        out_specs=[pl.BlockSpec((BB, TB, D), lambda i, j: (i, j, 0)),
                   pl.BlockSpec((BB, FFN_CONV_WIDTH - 1, d_ff), lambda i, j: (i, 0, 0))],
        out_shape=[jax.ShapeDtypeStruct((B, T, D), F32),
                   jax.ShapeDtypeStruct((B, FFN_CONV_WIDTH - 1, d_ff), F32)],
        scratch_shapes=[pltpu.VMEM((BB, FFN_PAD + TB, d_ff), F32)],
        compiler_params=pltpu.CompilerParams(dimension_semantics=("arbitrary", "arbitrary"),
                                             vmem_limit_bytes=VMEM_LIMIT),
        name="conv_ffn_ln",
    )(x, *params, s0)


def _row(v, width=None):
    v = v.reshape(1, -1).astype(F32)
    if width is not None and v.shape[1] < width:
        v = jnp.pad(v, ((0, 0), (0, width - v.shape[1])))
    return v


def _sgu_mats(sgu_w, sgu_b, chunk, rows):
    w = jnp.tril(sgu_w[:, :chunk, :chunk])
    w = jnp.pad(w, ((0, 0), (0, rows - chunk), (0, rows - chunk)))
    wcat = w.transpose(1, 0, 2).reshape(rows, N_HEADS * rows).astype(BF16)
    b = jnp.pad(sgu_b[:, :chunk], ((0, 0), (0, rows - chunk)))
    bias = jnp.repeat(b.T, HEAD_DIM, axis=1).astype(F32)
    return wcat, bias


def _layer_weights(l, p, sgu_chunks):
    w_o = p["w_o"][l]
    wts = dict(
        w_in=_pack_w_in(p["w_in"][l]),
        w_a=jnp.pad(p["gla_w_a"][l], ((0, LANE - GLA_RANK), (0, 0))).astype(BF16),
        b_a=_row(p["gla_b_a"][l]), gn=_row(p["gla_norm_g"][l]),
        sgu_g=_row(p["sgu_ln_g"][l]), sgu_b=_row(p["sgu_ln_b"][l]),
        bf=_row(p["fox_b_f"][l], LANE),
        cw=jnp.pad(p["conv_w"][l], ((0, CONV_PAD - CONV_WIDTH), (0, 0))).astype(F32),
        cb=_row(p["conv_b"][l]), cng=_row(p["conv_norm_g"][l]), cnb=_row(p["conv_norm_b"][l]),
        w_o_abd=jnp.concatenate([w_o[0:2 * D_GROUP], w_o[3 * D_GROUP:]], axis=0).astype(BF16),
        w_o_c=w_o[2 * D_GROUP:3 * D_GROUP].astype(BF16),
        ln1_g=_row(p["ln1_g"][l]), ln1_b=_row(p["ln1_b"][l]),
        w_up=p["ffn_w_up"][l].astype(BF16),
        ffn_cw=jnp.pad(p["ffn_conv_w"][l], ((0, 8 - FFN_CONV_WIDTH), (0, 0))).astype(F32),
        ffn_cb=_row(p["ffn_conv_b"][l]),
        w_dn=p["ffn_w_down"][l].astype(BF16),
        ln2_g=_row(p["ln2_g"][l]), ln2_b=_row(p["ln2_b"][l]),
        wcat={}, sbias={})
    for rows, chunk in sgu_chunks:
        wts["wcat"][rows], wts["sbias"][rows] = _sgu_mats(p["sgu_w"][l], p["sgu_b"][l], chunk, rows)
    return wts


def _gla_state_in(s):
    B = s.shape[0]
    eye = jnp.eye(N_HEADS, dtype=s.dtype)
    st = s.transpose(0, 1, 3, 2)[:, :, :, None, :] * eye[None, :, None, :, None]
    return st.reshape(B, D_GROUP, GLA_LANES)


def _gla_state_out(st):
    B = st.shape[0]
    s5 = st.reshape(B, N_HEADS, HEAD_DIM, N_HEADS, GLA_DK)
    diag = jnp.stack([s5[:, h, :, h, :] for h in range(N_HEADS)], axis=1)
    return diag.transpose(0, 1, 3, 2)


def kernel(x_prompt, x_sample, cache_fox_k, cache_fox_v, cache_fox_logf, state_gla, state_conv, state_ffn_conv,
           page_table, w_in, gla_w_a, gla_b_a, gla_norm_g, sgu_ln_g, sgu_ln_b, sgu_w, sgu_b, fox_b_f, conv_w,
           conv_b, conv_norm_g, conv_norm_b, w_o, ln1_g, ln1_b, ffn_w_up, ffn_conv_w, ffn_conv_b, ffn_w_down,
           ln2_g, ln2_b):
    p = dict(w_in=w_in, gla_w_a=gla_w_a, gla_b_a=gla_b_a, gla_norm_g=gla_norm_g, sgu_ln_g=sgu_ln_g,
             sgu_ln_b=sgu_ln_b, sgu_w=sgu_w, sgu_b=sgu_b, fox_b_f=fox_b_f, conv_w=conv_w, conv_b=conv_b,
             conv_norm_g=conv_norm_g, conv_norm_b=conv_norm_b, w_o=w_o, ln1_g=ln1_g, ln1_b=ln1_b,
             ffn_w_up=ffn_w_up, ffn_conv_w=ffn_conv_w, ffn_conv_b=ffn_conv_b, ffn_w_down=ffn_w_down,
             ln2_g=ln2_g, ln2_b=ln2_b)
    depth = w_in.shape[0]
    alpha = float((2 * depth) ** 0.25)
    B, T, D = x_prompt.shape
    SB, S, _ = x_sample.shape
    d_ff = ffn_conv_w.shape[-1]
    n_pool = cache_fox_k.shape[1]
    TB = 256
    LG, LS = 64, 128
    SR = SAMPLE_ROWS
    PG = 8

    xp = x_prompt
    xs = jnp.pad(x_sample, ((0, 0), (0, SR - S), (0, 0)))
    ck = cache_fox_k.transpose(0, 1, 3, 4, 2).reshape(depth, n_pool, D_GROUP, PAGE)
    cv = cache_fox_v.transpose(0, 1, 3, 4, 2).reshape(depth, n_pool, D_GROUP, PAGE)
    zeros_gla = jnp.zeros((B, D_GROUP, GLA_LANES), F32)
    zeros_conv = jnp.zeros((B, CONV_PAD, D_GROUP), F32)
    zeros_ffn = jnp.zeros((B, FFN_PAD, d_ff), F32)
    st_p, st_s = [], []
    for l in range(depth):
        wts = _layer_weights(l, p, ((LS, LS), (SR, S)))

        (y, fq, fk, fv, fkb, fvb, logf, dq, dkt, gla_st, conv_st) = _mixer_call(
            xp, wts, zeros_gla, zeros_conv, BB=1, TB=TB, LG=LG, LS=LS, t_valid_last=TB, want_vln=False, kv_t=True)
        yc = _flash_call(fq, fkb, fvb, dq, dkt, TQ=TB)
        x1 = _outproj_call(y.reshape(B * T, -1), yc.reshape(B * T, -1), xp.reshape(B * T, D), wts,
                           alpha=alpha, TR=512).reshape(B, T, D)
        xp, ffn_st = _ffn_call(x1, wts, zeros_ffn, alpha=alpha, BB=1, TB=TB, t_valid_last=TB)
        st_p.append((fk.reshape(B, N_HEADS, HEAD_DIM, T).transpose(0, 3, 1, 2),
                     fv.reshape(B, N_HEADS, HEAD_DIM, T).transpose(0, 3, 1, 2), logf,
                     _gla_state_out(gla_st), conv_st, ffn_st))

        s0_gla = _gla_state_in(state_gla[l])
        s0_conv = jnp.pad(state_conv[l], ((0, 0), (CONV_PAD - (CONV_WIDTH - 1), 0), (0, 0)))
        s0_ffn = jnp.pad(state_ffn_conv[l], ((0, 0), (FFN_PAD - (FFN_CONV_WIDTH - 1), 0), (0, 0)))
        (y, fq, fk, fv, fkb, fvb, logf, dq, dkt, gla_st, conv_st, vln) = _mixer_call(
            xs, wts, s0_gla, s0_conv, BB=SB, TB=SR, LG=SR, LS=SR, t_valid_last=S, want_vln=True, kv_t=False)
        within, tot = _logf_prep_call(cache_fox_logf[l].transpose(0, 2, 1).reshape(n_pool, N_HEADS * PAGE))
        k_new = jnp.pad(fkb, ((0, 0), (0, PAGE - SR), (0, 0)))
        v_new = jnp.pad(fvb, ((0, 0), (0, PAGE - SR), (0, 0)))
        dnt = jnp.pad(dkt[:, 0], ((0, 0), (0, 0), (0, PAGE - SR)))
        yc = _paged_call(page_table, fq, dq, k_new, v_new, dnt, ck, cv, within, tot, l, PG=PG)
        x1 = _outproj_call(y.reshape(SB * SR, -1), yc.reshape(SB * SR, -1), xs.reshape(SB * SR, D), wts,
                           alpha=alpha, TR=SB * SR).reshape(SB, SR, D)
        xs, ffn_st = _ffn_call(x1, wts, s0_ffn, alpha=alpha, BB=SB, TB=SR, t_valid_last=S)
        st_s.append((fk[:, :S].reshape(SB, S, N_HEADS, HEAD_DIM), fv[:, :S].reshape(SB, S, N_HEADS, HEAD_DIM),
                     logf[:, :S], _gla_state_out(gla_st), conv_st, ffn_st, vln[:, :S]))

    def stack(states, i):
        return jnp.stack([s[i] for s in states])

    return (xp, xs[:, :S], stack(st_p, 0), stack(st_p, 1), stack(st_p, 2), stack(st_p, 3), stack(st_p, 4),
            stack(st_p, 5), stack(st_s, 0), stack(st_s, 1), stack(st_s, 2), stack(st_s, 3), stack(st_s, 4),
            stack(st_s, 5), stack(st_s, 6))
```
